```python
import math
import jax, jax.numpy as jnp
from jax import lax
import numpy as np

D_MODEL = 1024
BATCH = 4
SEQ = 8192
DEPTH = 4

HEAD_DIM = 64
N_HEADS = D_MODEL // HEAD_DIM
SB_HEADS = N_HEADS // 2
MOBA_HEADS = N_HEADS - SB_HEADS
SWA_HEADS = N_HEADS
SWA_KV_HEADS = max(1, N_HEADS // 8)
WINDOW = 128
SB_BLOCK = 128
MOBA_BLOCK = 256
MOBA_TOPK = 3
MOBA_Q_CHUNK = 32
NUM_BUCKETS = 32
MAX_DISTANCE = 4096
MEM_LEN = 256
CROSS_HEADS = 4
CROSS_HEAD_DIM = 128
D_FF = ((8 * D_MODEL // 3 + 127) // 128) * 128
CONV_WIDTH = 3
N_EVEN = (DEPTH + 1) // 2
N_ODD = DEPTH // 2
RMS_EPS = 1e-6
NEG_INF = -1e30

kernel_name = 'hybrid_stickbreak_moba_swa_trunk'


def rms_norm(x, gain):
    xf = x.astype(jnp.float32)
    y = xf * lax.rsqrt(jnp.mean(xf * xf, axis=-1, keepdims=True) + RMS_EPS)
    return (y * gain.astype(jnp.float32)).astype(x.dtype)


def t5_bucket(dist):
    n = jnp.maximum(dist, 0)
    max_exact = NUM_BUCKETS // 2
    nf = jnp.maximum(n, 1).astype(jnp.float32)
    coef = (NUM_BUCKETS - max_exact) / math.log(MAX_DISTANCE / max_exact)
    large = max_exact + (jnp.log(nf / max_exact) * coef).astype(jnp.int32)
    large = jnp.minimum(large, NUM_BUCKETS - 1)
    return jnp.where(n < max_exact, n, large)


def stick_breaking_attention(q, k, v):
    B, H, S, D = q.shape
    scale = D ** -0.5
    pos_k = jnp.arange(S)
    vf = v.astype(jnp.float32)

    def block(c):
        start = c * SB_BLOCK
        qc = lax.dynamic_slice_in_dim(q, start, SB_BLOCK, axis=2)
        z = jnp.einsum('bhqd,bhkd->bhqk', qc, k).astype(jnp.float32) * scale
        pos_q = start + jnp.arange(SB_BLOCK)
        past = pos_k[None, :] < pos_q[:, None]
        log_not = jnp.where(past, jax.nn.log_sigmoid(-z), 0.0)
        between = lax.cumsum(log_not, axis=3, reverse=True) - log_not
        w = jnp.where(past, jnp.exp(jax.nn.log_sigmoid(z) + between), 0.0)
        return jnp.einsum('bhqk,bhkd->bhqd', w, vf).astype(q.dtype)

    out = lax.map(block, jnp.arange(S // SB_BLOCK))
    return jnp.moveaxis(out, 0, 2).reshape(B, H, S, D)


def moba_attention(q, k, v, bias_table):
    B, H, S, D = q.shape
    scale = D ** -0.5
    nb = -(-S // MOBA_BLOCK)
    pad = nb * MOBA_BLOCK - S
    kb = jnp.pad(k, ((0, 0), (0, 0), (0, pad), (0, 0))).reshape(B, H, nb, MOBA_BLOCK, D)
    vb = jnp.pad(v, ((0, 0), (0, 0), (0, pad), (0, 0))).reshape(B, H, nb, MOBA_BLOCK, D)
    k_mean = jnp.mean(kb.astype(jnp.float32), axis=3)
    n_sel = max(1, min(MOBA_TOPK, nb))
    bias_hb = bias_table.T.astype(jnp.float32)
    head_idx = jnp.arange(H)[:, None, None, None]
    offs = jnp.arange(MOBA_BLOCK)
    gather = jax.vmap(jax.vmap(lambda blocks, idx: blocks[idx]))
    n_flat = n_sel * MOBA_BLOCK

    def chunk(c):
        start = c * MOBA_Q_CHUNK
        own = start // MOBA_BLOCK
        pos_q = start + jnp.arange(MOBA_Q_CHUNK)
        qc = lax.dynamic_slice_in_dim(q, start, MOBA_Q_CHUNK, axis=2)
        gate = jnp.einsum('bhqd,bhnd->bhqn', qc.astype(jnp.float32), k_mean)
        gate = jnp.where(jnp.arange(nb) < own, gate, NEG_INF)
        _, idx = lax.top_k(gate, n_sel)
        valid = jnp.arange(n_sel) < own
        kg = gather(kb, idx)
        vg = gather(vb, idx)
        dist_sel = pos_q[:, None, None] - (idx[..., None] * MOBA_BLOCK + offs)
        s_sel = (jnp.einsum('bhqd,bhqnld->bhqnl', qc, kg).astype(jnp.float32) * scale
                 + bias_hb[head_idx, t5_bucket(dist_sel)])
        s_sel = jnp.where(valid[:, None], s_sel, NEG_INF).reshape(B, H, MOBA_Q_CHUNK, n_flat)
        k_own = lax.dynamic_index_in_dim(kb, own, axis=2, keepdims=False)
        v_own = lax.dynamic_index_in_dim(vb, own, axis=2, keepdims=False)
        dist_own = pos_q[:, None] - (own * MOBA_BLOCK + offs)[None, :]
        s_own = (jnp.einsum('bhqd,bhld->bhql', qc, k_own).astype(jnp.float32) * scale
                 + bias_hb[:, t5_bucket(dist_own)])
        s_own = jnp.where(dist_own >= 0, s_own, NEG_INF)
        p = jax.nn.softmax(jnp.concatenate([s_sel, s_own], axis=-1), axis=-1)
        p_sel = p[..., :n_flat].reshape(B, H, MOBA_Q_CHUNK, n_sel, MOBA_BLOCK)
        p_own = p[..., n_flat:]
        out = (jnp.einsum('bhqnl,bhqnld->bhqd', p_sel, vg.astype(jnp.float32))
               + jnp.einsum('bhql,bhld->bhqd', p_own, v_own.astype(jnp.float32)))
        return out.astype(q.dtype)

    out = lax.map(chunk, jnp.arange(S // MOBA_Q_CHUNK))
    return jnp.moveaxis(out, 0, 2).reshape(B, H, S, D)


def swa_sink_attention(q, k, v, sinks, bias_table):
    B, S, Hq, D = q.shape
    Hkv = k.shape[2]
    G = Hq // Hkv
    blk = WINDOW
    nqb = S // blk
    scale = D ** -0.5
    qb = q.reshape(B, nqb, blk, Hkv, G, D)

    def band(t):
        tb = t.reshape(B, nqb, blk, Hkv, D)
        prev = jnp.pad(tb, ((0, 0), (1, 0), (0, 0), (0, 0), (0, 0)))[:, :-1]
        return jnp.concatenate([prev, tb], axis=2)

    kk, vv = band(k), band(v)
    qi = jnp.arange(blk)
    kj = jnp.arange(2 * blk)
    dist = qi[:, None] + blk - kj[None, :]
    in_window = (dist >= 0) & (dist < WINDOW)
    key_exists = (jnp.arange(nqb)[:, None, None] * blk - blk + kj[None, None, :]) >= 0
    mask = in_window[None] & key_exists
    bias = bias_table.astype(jnp.float32)[t5_bucket(dist)]
    bias = jnp.transpose(bias, (2, 0, 1)).reshape(Hkv, G, 1, blk, 2 * blk)
    s = jnp.einsum('bnqhgd,bnkhd->bhgnqk', qb, kk).astype(jnp.float32) * scale + bias
    s = jnp.where(mask, s, NEG_INF)
    sink = sinks.astype(jnp.float32).reshape(Hkv, G)[:, :, None, None, None]
    m = jnp.maximum(jnp.max(s, axis=-1, keepdims=True), sink)
    p = jnp.exp(s - m)
    denom = jnp.sum(p, axis=-1, keepdims=True) + jnp.exp(sink - m)
    out = jnp.einsum('bhgnqk,bnkhd->bnqhgd', p / denom, vv.astype(jnp.float32))
    return out.reshape(B, S, Hq * D).astype(q.dtype)


def even_mixer(h, w_in, w_out, q_gain, k_gain, bias_table):
    B, S, _ = h.shape
    qa, ka, va, qm, km, vm = jnp.split(h @ w_in, 6, axis=-1)

    def heads(t, n):
        return t.reshape(B, S, n, HEAD_DIM).transpose(0, 2, 1, 3)

    out_a = stick_breaking_attention(heads(qa, SB_HEADS), heads(ka, SB_HEADS), heads(va, SB_HEADS))
    qm = rms_norm(heads(qm, MOBA_HEADS), q_gain)
    km = rms_norm(heads(km, MOBA_HEADS), k_gain)
    out_b = moba_attention(qm, km, heads(vm, MOBA_HEADS), bias_table[:, SB_HEADS:])
    out = jnp.concatenate([out_a, out_b], axis=1).transpose(0, 2, 1, 3).reshape(B, S, N_HEADS * HEAD_DIM)
    return out @ w_out


def odd_mixer(h, w_in, w_out, q_gain, k_gain, sinks, bias_table):
    B, S, _ = h.shape
    q, k, v = jnp.split(h @ w_in, [SWA_HEADS * HEAD_DIM, (SWA_HEADS + SWA_KV_HEADS) * HEAD_DIM], axis=-1)
    q = rms_norm(q.reshape(B, S, SWA_HEADS, HEAD_DIM), q_gain)
    k = rms_norm(k.reshape(B, S, SWA_KV_HEADS, HEAD_DIM), k_gain)
    v = v.reshape(B, S, SWA_KV_HEADS, HEAD_DIM)
    return swa_sink_attention(q, k, v, sinks, bias_table) @ w_out


def memory_cross_attention(h, mem_n, w_q, w_kv, w_o, q_gain, k_gain):
    B, S, _ = h.shape
    M = mem_n.shape[1]
    q = rms_norm((h @ w_q).reshape(B, S, CROSS_HEADS, CROSS_HEAD_DIM), q_gain)
    k, v = jnp.split(mem_n @ w_kv, 2, axis=-1)
    k = rms_norm(k.reshape(B, M, CROSS_HEADS, CROSS_HEAD_DIM), k_gain)
    v = v.reshape(B, M, CROSS_HEADS, CROSS_HEAD_DIM)
    s = jnp.einsum('bshd,bmhd->bhsm', q, k).astype(jnp.float32) * (CROSS_HEAD_DIM ** -0.5)
    p = jax.nn.softmax(s, axis=-1)
    o = jnp.einsum('bhsm,bmhd->bshd', p, v.astype(jnp.float32)).astype(h.dtype)
    return o.reshape(B, S, CROSS_HEADS * CROSS_HEAD_DIM) @ w_o


def conv_ffn(h, w_in, conv_w, conv_b, w_out):
    S = h.shape[1]
    u = h @ w_in
    u_pad = jnp.pad(u, ((0, 0), (CONV_WIDTH - 1, 0), (0, 0)))
    c = conv_b
    for i in range(CONV_WIDTH):
        c = c + conv_w[i] * u_pad[:, i:i + S]
    gate, up = jnp.split(c, 2, axis=-1)
    return (jax.nn.silu(gate) * up) @ w_out


def setup_inputs(seed: int = 0) -> dict:
    key = jax.random.key(seed)
    ks = jax.random.split(key, 32)
    f32 = jnp.float32

    def nrm(k, shape, scale):
        return jax.random.normal(k, shape, f32) * scale

    def gain(k, shape):
        return 1.0 + 0.05 * jax.random.normal(k, shape, f32)

    D = D_MODEL
    mix_w = N_HEADS * HEAD_DIM
    ev_in = 3 * (SB_HEADS + MOBA_HEADS) * HEAD_DIM
    od_in = (SWA_HEADS + 2 * SWA_KV_HEADS) * HEAD_DIM
    cw = CROSS_HEADS * CROSS_HEAD_DIM
    return {
        'x': nrm(ks[0], (BATCH, SEQ, D), 1.0),
        'mem': nrm(ks[1], (BATCH, MEM_LEN, D), 1.0),
        'rel_bias': nrm(ks[2], (NUM_BUCKETS, N_HEADS), 0.2),
        'mix_norm': gain(ks[3], (DEPTH, D)),
        'ev_w_in': nrm(ks[4], (N_EVEN, D, ev_in), D ** -0.5),
        'ev_w_out': nrm(ks[5], (N_EVEN, mix_w, D), 0.5 * mix_w ** -0.5),
        'ev_q_gain': gain(ks[6], (N_EVEN, HEAD_DIM)),
        'ev_k_gain': gain(ks[7], (N_EVEN, HEAD_DIM)),
        'od_w_in': nrm(ks[8], (N_ODD, D, od_in), D ** -0.5),
        'od_w_out': nrm(ks[9], (N_ODD, mix_w, D), 0.5 * mix_w ** -0.5),
        'od_q_gain': gain(ks[10], (N_ODD, HEAD_DIM)),
        'od_k_gain': gain(ks[11], (N_ODD, HEAD_DIM)),
        'od_sinks': nrm(ks[12], (N_ODD, SWA_HEADS), 0.5),
        'cx_norm': gain(ks[13], (DEPTH, D)),
        'cx_mem_norm': gain(ks[14], (DEPTH, D)),
        'cx_w_q': nrm(ks[15], (DEPTH, D, cw), D ** -0.5),
        'cx_w_kv': nrm(ks[16], (DEPTH, D, 2 * cw), D ** -0.5),
        'cx_w_o': nrm(ks[17], (DEPTH, cw, D), 0.5 * cw ** -0.5),
        'cx_q_gain': gain(ks[18], (DEPTH, CROSS_HEAD_DIM)),
        'cx_k_gain': gain(ks[19], (DEPTH, CROSS_HEAD_DIM)),
        'ff_norm': gain(ks[20], (DEPTH, D)),
        'ff_w_in': nrm(ks[21], (DEPTH, D, 2 * D_FF), D ** -0.5),
        'ff_conv_w': nrm(ks[22], (DEPTH, CONV_WIDTH, 2 * D_FF), CONV_WIDTH ** -0.5),
        'ff_conv_b': nrm(ks[23], (DEPTH, 2 * D_FF), 0.02),
        'ff_w_out': nrm(ks[24], (DEPTH, D_FF, D), 0.5 * D_FF ** -0.5),
    }


def reference(x, mem, rel_bias, mix_norm, ev_w_in, ev_w_out, ev_q_gain, ev_k_gain,
              od_w_in, od_w_out, od_q_gain, od_k_gain, od_sinks,
              cx_norm, cx_mem_norm, cx_w_q, cx_w_kv, cx_w_o, cx_q_gain, cx_k_gain,
              ff_norm, ff_w_in, ff_conv_w, ff_conv_b, ff_w_out):
    for layer in range(DEPTH):
        i = layer // 2
        h = rms_norm(x, mix_norm[layer])
        if layer % 2 == 0:
            x = x + even_mixer(h, ev_w_in[i], ev_w_out[i], ev_q_gain[i], ev_k_gain[i], rel_bias)
        else:
            x = x + odd_mixer(h, od_w_in[i], od_w_out[i], od_q_gain[i], od_k_gain[i],
                              od_sinks[i], rel_bias)
        x = x + memory_cross_attention(rms_norm(x, cx_norm[layer]), rms_norm(mem, cx_mem_norm[layer]),
                                       cx_w_q[layer], cx_w_kv[layer], cx_w_o[layer],
                                       cx_q_gain[layer], cx_k_gain[layer])
        x = x + conv_ffn(rms_norm(x, ff_norm[layer]), ff_w_in[layer], ff_conv_w[layer],
                         ff_conv_b[layer], ff_w_out[layer])
    return x
```

```python
import functools
import math

import jax
import jax.numpy as jnp
import numpy as np
from jax import lax
from jax.experimental import pallas as pl
from jax.experimental.pallas import tpu as pltpu

F32 = jnp.float32
BF16 = jnp.bfloat16

HEAD_DIM = 64
PAIR = 2 * HEAD_DIM
TB = 256
SWA_WINDOW = 128
MOBA_TOPK = 3
NUM_BUCKETS = 32
MAX_DISTANCE = 4096
CROSS_HEADS = 4
CROSS_HEAD_DIM = 128
CONV_WIDTH = 3
RMS_EPS = 1e-6
NEG_INF = -1e30
SB_DEAD_LOG = -104.0
FFN_CHUNK = 256
FFN_TILE = 512
POST_TILE = 512
VMEM_LIMIT = 56 * 1024 * 1024

_NT = (((1,), (1,)), ((), ()))
_TN = (((0,), (0,)), ((), ()))


def _rms_rows(x, gain):
    ms = jnp.mean(x * x, axis=-1, keepdims=True)
    return (x * lax.rsqrt(ms + RMS_EPS)) * gain


def _rms_scale_cols(xt):
    return lax.rsqrt(jnp.mean(xt * xt, axis=0, keepdims=True) + RMS_EPS)


def _params(n_axes):
    return pltpu.CompilerParams(dimension_semantics=("arbitrary",) * n_axes,
                                vmem_limit_bytes=VMEM_LIMIT)


def _pair_extend(q, head):
    odd = (head % 2).astype(F32)
    q32 = q.astype(F32)
    return jnp.concatenate([q32 * (1.0 - odd), q32 * odd], axis=0).astype(BF16)


def _even_inproj_kernel(x_ref, g_ref, wt_ref, kg_ref,
                        qa_ref, ka_ref, va_ref, qb_ref, kb_ref, vb_ref):
    h = _rms_rows(x_ref[...], g_ref[...]).astype(BF16)
    width = 8 * HEAD_DIM

    def proj(group):
        w = wt_ref[group * width:(group + 1) * width, :]
        return lax.dot_general(w, h, _NT, preferred_element_type=F32)

    def heads(t):
        return [t[hh * HEAD_DIM:(hh + 1) * HEAD_DIM, :] for hh in range(8)]

    scale = HEAD_DIM ** -0.5
    for hh, blk in enumerate(heads(proj(0))):
        qa_ref[hh, 0] = (blk * scale).astype(BF16)
    ka_ref[...] = proj(1).T.astype(BF16)
    for hh, blk in enumerate(heads(proj(2))):
        va_ref[hh, 0] = blk.astype(BF16)
    for hh, blk in enumerate(heads(proj(3))):
        qb_ref[hh, 0] = (blk * (_rms_scale_cols(blk) * scale)).astype(BF16)
    kb = jnp.concatenate([blk * _rms_scale_cols(blk) for blk in heads(proj(4))], axis=0)
    kb_ref[...] = (kb * kg_ref[...]).T.astype(BF16)
    for hh, blk in enumerate(heads(proj(5))):
        vb_ref[hh, 0] = blk.astype(BF16)


def _even_inproj(x, gain, wt, kgain):
    T, D = x.shape
    nblk = T // TB
    blocked = jax.ShapeDtypeStruct((8, nblk, HEAD_DIM, TB), BF16)
    rowmajor = jax.ShapeDtypeStruct((T, 8 * HEAD_DIM), BF16)
    bspec = pl.BlockSpec((8, 1, HEAD_DIM, TB), lambda i: (0, i, 0, 0))
    rspec = pl.BlockSpec((TB, 8 * HEAD_DIM), lambda i: (i, 0))
    return pl.pallas_call(
        _even_inproj_kernel,
        grid=(nblk,),
        in_specs=[pl.BlockSpec((TB, D), lambda i: (i, 0)),
                  pl.BlockSpec((1, D), lambda i: (0, 0)),
                  pl.BlockSpec(wt.shape, lambda i: (0, 0)),
                  pl.BlockSpec(kgain.shape, lambda i: (0, 0))],
        out_specs=[bspec, rspec, bspec, bspec, rspec, bspec],
        out_shape=[blocked, rowmajor, blocked, blocked, rowmajor, blocked],
        compiler_params=_params(1),
        name="even_inproj",
    )(x, gain, wt, kgain)


def _odd_inproj_kernel(x_ref, g_ref, wt_ref, kg_ref, q_ref, k_ref, v_ref, *, n_q, n_kv):
    h = _rms_rows(x_ref[...], g_ref[...]).astype(BF16)
    scale = HEAD_DIM ** -0.5
    group = 8 * HEAD_DIM
    for g in range(n_q * HEAD_DIM // group):
        w = wt_ref[g * group:(g + 1) * group, :]
        t = lax.dot_general(w, h, _NT, preferred_element_type=F32)
        for hh in range(8):
            blk = t[hh * HEAD_DIM:(hh + 1) * HEAD_DIM, :]
            q_ref[g * 8 + hh, 0] = (blk * (_rms_scale_cols(blk) * scale)).astype(BF16)
    kv_rows = n_kv * HEAD_DIM
    w = wt_ref[n_q * HEAD_DIM:n_q * HEAD_DIM + 2 * kv_rows, :]
    t = lax.dot_general(w, h, _NT, preferred_element_type=F32)
    kparts = []
    for hh in range(n_kv):
        blk = t[hh * HEAD_DIM:(hh + 1) * HEAD_DIM, :]
        kparts.append(blk * _rms_scale_cols(blk))
        v_ref[hh, 0] = t[kv_rows + hh * HEAD_DIM:kv_rows + (hh + 1) * HEAD_DIM, :].astype(BF16)
    k_ref[...] = (jnp.concatenate(kparts, axis=0) * kg_ref[...]).T.astype(BF16)


def _odd_inproj(x, gain, wt, kgain, n_q, n_kv):
    T, D = x.shape
    nblk = T // TB
    return pl.pallas_call(
        functools.partial(_odd_inproj_kernel, n_q=n_q, n_kv=n_kv),
        grid=(nblk,),
        in_specs=[pl.BlockSpec((TB, D), lambda i: (i, 0)),
                  pl.BlockSpec((1, D), lambda i: (0, 0)),
                  pl.BlockSpec(wt.shape, lambda i: (0, 0)),
                  pl.BlockSpec(kgain.shape, lambda i: (0, 0))],
        out_specs=[pl.BlockSpec((n_q, 1, HEAD_DIM, TB), lambda i: (0, i, 0, 0)),
                   pl.BlockSpec((TB, n_kv * HEAD_DIM), lambda i: (i, 0)),
                   pl.BlockSpec((n_kv, 1, HEAD_DIM, TB), lambda i: (0, i, 0, 0))],
        out_shape=[jax.ShapeDtypeStruct((n_q, nblk, HEAD_DIM, TB), BF16),
                   jax.ShapeDtypeStruct((T, n_kv * HEAD_DIM), BF16),
                   jax.ShapeDtypeStruct((n_kv, nblk, HEAD_DIM, TB), BF16)],
        compiler_params=_params(1),
        name="odd_inproj",
    )(x, gain, wt, kgain)


def _sb_kernel(q_ref, k_ref, v_ref, o_ref):
    head = pl.program_id(1)
    i = pl.program_id(2)
    q_ext = _pair_extend(q_ref[0, 0], head)
    key_idx = lax.broadcasted_iota(jnp.int32, (TB, TB), 0)
    qry_idx = lax.broadcasted_iota(jnp.int32, (TB, TB), 1)
    strictly_later = jnp.where(qry_idx > key_idx, 1.0, 0.0).astype(BF16)
    past = key_idx < qry_idx

    def block(j, carry, acc, diagonal):
        kj = k_ref[pl.ds(pl.multiple_of(j * TB, TB), TB), :]
        z = jnp.dot(kj, q_ext, preferred_element_type=F32)
        softplus_neg_abs = jnp.log(1.0 + jnp.exp(-jnp.abs(z)))
        log_beta = jnp.minimum(z, 0.0) - softplus_neg_abs
        log_not = -jnp.maximum(z, 0.0) - softplus_neg_abs
        if diagonal:
            log_not = jnp.where(past, log_not, 0.0)
        hi = log_not.astype(BF16)
        lo = (log_not - hi.astype(F32)).astype(BF16)
        between = (jnp.dot(strictly_later, hi, preferred_element_type=F32)
                   + jnp.dot(strictly_later, lo, preferred_element_type=F32))
        w = jnp.exp(log_beta + between + carry)
        if diagonal:
            w = jnp.where(past, w, 0.0)
        acc = acc + jnp.dot(v_ref[0, j], w.astype(BF16), preferred_element_type=F32)
        carry = carry + jnp.sum(log_not, axis=0, keepdims=True)
        return carry, acc

    carry, acc = block(i, jnp.zeros((1, TB), F32), jnp.zeros((HEAD_DIM, TB), F32), True)

    def alive_flag(c):
        return (jnp.max(c) > SB_DEAD_LOG).astype(jnp.int32)

    def cond(state):
        return jnp.logical_and(state[0] >= 0, state[1] > 0)

    def body(state):
        j, _, c, a = state
        c, a = block(j, c, a, False)
        return j - 1, alive_flag(c), c, a

    _, _, _, acc = lax.while_loop(cond, body, (i - 1, alive_flag(carry), carry, acc))
    o_ref[...] = acc.astype(BF16)


def _sb_attention(qt, k, vt, batch):
    n_heads, nblk = qt.shape[0], qt.shape[1]
    nq = nblk // batch
    S = nq * TB
    return pl.pallas_call(
        _sb_kernel,
        grid=(batch, n_heads, nq),
        in_specs=[pl.BlockSpec((1, 1, HEAD_DIM, TB), lambda b, h, i: (h, b * nq + i, 0, 0)),
                  pl.BlockSpec((S, PAIR), lambda b, h, i: (b, h // 2)),
                  pl.BlockSpec((1, nq, HEAD_DIM, TB), lambda b, h, i: (h, b, 0, 0))],
        out_specs=pl.BlockSpec((HEAD_DIM, TB), lambda b, h, i: (h, b * nq + i)),
        out_shape=jax.ShapeDtypeStruct((n_heads * HEAD_DIM, nblk * TB), BF16),
        compiler_params=_params(3),
        name="stickbreak_attn",
    )(qt, k, vt)


def _moba_kernel(q_ref, k_ref, v_ref, bias_ref, o_ref, kmean_ref, sel_ref, *, n_tiles):
    head = pl.program_id(0)
    i = pl.program_id(2)
    nb = kmean_ref.shape[0]
    q_ext = _pair_extend(q_ref[0, 0], head)

    @pl.when(i == 0)
    def _():
        def one(jb, _):
            blk = k_ref[pl.ds(pl.multiple_of(jb * TB, TB), TB), :].astype(F32)
            kmean_ref[pl.ds(jb, 1), :] = jnp.mean(blk, axis=0, keepdims=True)
            return 0
        lax.fori_loop(0, nb, one, 0)

    km = kmean_ref[...]
    km_hi = km.astype(BF16)
    km_lo = (km - km_hi.astype(F32)).astype(BF16)
    gate = (jnp.dot(km_hi, q_ext, preferred_element_type=F32)
            + jnp.dot(km_lo, q_ext, preferred_element_type=F32))
    blk_idx = lax.broadcasted_iota(jnp.int32, (nb, TB), 0)
    gate = jnp.where(blk_idx < i, gate, NEG_INF)
    sel = jnp.zeros((nb, TB), F32)
    for _ in range(MOBA_TOPK):
        best = jnp.max(gate, axis=0, keepdims=True)
        cand = jnp.where(jnp.logical_and(gate == best, best > 0.5 * NEG_INF), blk_idx, nb)
        pick = blk_idx == jnp.min(cand, axis=0, keepdims=True)
        sel = jnp.where(pick, 1.0, sel)
        gate = jnp.where(pick, NEG_INF, gate)
    sel_ref[...] = sel

    def scores(j, tile):
        kj = k_ref[pl.ds(pl.multiple_of(j * TB, TB), TB), :]
        return jnp.dot(kj, q_ext, preferred_element_type=F32) + bias_ref[0, tile]

    def update(j, s, state):
        m, l, acc = state
        m_new = jnp.maximum(m, jnp.max(s, axis=0, keepdims=True))
        alpha = jnp.exp(m - m_new)
        p = jnp.exp(s - m_new)
        l = alpha * l + jnp.sum(p, axis=0, keepdims=True)
        acc = alpha * acc + jnp.dot(v_ref[0, j], p.astype(BF16), preferred_element_type=F32)
        return m_new, l, acc

    def past_block(j, state):
        s = scores(j, jnp.minimum(i - j, n_tiles - 1))
        s = jnp.where(sel_ref[pl.ds(j, 1), :] > 0.0, s, NEG_INF)
        return update(j, s, state)

    state = (jnp.full((1, TB), NEG_INF, F32), jnp.zeros((1, TB), F32),
             jnp.zeros((HEAD_DIM, TB), F32))
    state = lax.fori_loop(0, i, past_block, state)
    _, l, acc = update(i, scores(i, 0), state)
    o_ref[...] = (acc / l).astype(BF16)


def _moba_attention(qt, k, vt, bias_tiles, batch):
    n_heads, nblk = qt.shape[0], qt.shape[1]
    nq = nblk // batch
    S = nq * TB
    n_tiles = bias_tiles.shape[1]
    return pl.pallas_call(
        functools.partial(_moba_kernel, n_tiles=n_tiles),
        grid=(n_heads, batch, nq),
        in_specs=[pl.BlockSpec((1, 1, HEAD_DIM, TB), lambda h, b, i: (h, b * nq + i, 0, 0)),
                  pl.BlockSpec((S, PAIR), lambda h, b, i: (b, h // 2)),
                  pl.BlockSpec((1, nq, HEAD_DIM, TB), lambda h, b, i: (h, b, 0, 0)),
                  pl.BlockSpec((1, n_tiles, TB, TB), lambda h, b, i: (h, 0, 0, 0))],
        out_specs=pl.BlockSpec((HEAD_DIM, TB), lambda h, b, i: (h, b * nq + i)),
        out_shape=jax.ShapeDtypeStruct((n_heads * HEAD_DIM, nblk * TB), BF16),
        scratch_shapes=[pltpu.VMEM((nq, PAIR), F32), pltpu.VMEM((nq, TB), F32)],
        compiler_params=_params(3),
        name="moba_attn",
    )(qt, k, vt, bias_tiles)


def _swa_kernel(sink_ref, q_ref, kp_ref, kc_ref, vp_ref, vc_ref, bias_ref, o_ref, *, group):
    head = pl.program_id(0)
    i = pl.program_id(2)
    q_ext = _pair_extend(q_ref[0, 0], head // group)
    half = TB - SWA_WINDOW
    s_prev = (jnp.dot(kp_ref[half:, :], q_ext, preferred_element_type=F32)
              + bias_ref[0, :SWA_WINDOW, :])
    s_prev = jnp.where(i > 0, s_prev, NEG_INF)
    s_cur = jnp.dot(kc_ref[...], q_ext, preferred_element_type=F32) + bias_ref[0, SWA_WINDOW:, :]
    sink = sink_ref[head]
    m = jnp.maximum(jnp.maximum(jnp.max(s_prev, axis=0, keepdims=True),
                                jnp.max(s_cur, axis=0, keepdims=True)), sink)
    p_prev = jnp.exp(s_prev - m)
    p_cur = jnp.exp(s_cur - m)
    denom = (jnp.sum(p_prev, axis=0, keepdims=True) + jnp.sum(p_cur, axis=0, keepdims=True)
             + jnp.exp(sink - m))
    out = (jnp.dot(vp_ref[0, 0][:, half:], p_prev.astype(BF16), preferred_element_type=F32)
           + jnp.dot(vc_ref[0, 0], p_cur.astype(BF16), preferred_element_type=F32))
    o_ref[...] = (out / denom).astype(BF16)


def _swa_attention(qt, k, vt, bias_tiles, sinks, batch):
    n_heads, nblk = qt.shape[0], qt.shape[1]
    n_kv = vt.shape[0]
    group = n_heads // n_kv
    nq = nblk // batch

    def prev(b, i):
        return b * nq + jnp.maximum(i - 1, 0)

    return pl.pallas_call(
        functools.partial(_swa_kernel, group=group),
        grid_spec=pltpu.PrefetchScalarGridSpec(
            num_scalar_prefetch=1,
            grid=(n_heads, batch, nq),
            in_specs=[
                pl.BlockSpec((1, 1, HEAD_DIM, TB), lambda h, b, i, s: (h, b * nq + i, 0, 0)),
                pl.BlockSpec((TB, PAIR), lambda h, b, i, s: (prev(b, i), h // (2 * group))),
                pl.BlockSpec((TB, PAIR), lambda h, b, i, s: (b * nq + i, h // (2 * group))),
                pl.BlockSpec((1, 1, HEAD_DIM, TB), lambda h, b, i, s: (h // group, prev(b, i), 0, 0)),
                pl.BlockSpec((1, 1, HEAD_DIM, TB), lambda h, b, i, s: (h // group, b * nq + i, 0, 0)),
                pl.BlockSpec((1, SWA_WINDOW + TB, TB), lambda h, b, i, s: (h, 0, 0)),
            ],
            out_specs=pl.BlockSpec((HEAD_DIM, TB), lambda h, b, i, s: (h, b * nq + i)),
        ),
        out_shape=jax.ShapeDtypeStruct((n_heads * HEAD_DIM, nblk * TB), BF16),
        compiler_params=_params(3),
        name="swa_attn",
    )(sinks, qt, k, k, vt, vt, bias_tiles)


def _mem_kv_kernel(mem_ref, g_ref, wt_ref, kg_ref, k_ref, v_ref):
    h = _rms_rows(mem_ref[0], g_ref[0]).astype(BF16)
    t = lax.dot_general(wt_ref[0], h, _NT, preferred_element_type=F32)
    cw = CROSS_HEADS * CROSS_HEAD_DIM
    for hh in range(CROSS_HEADS):
        blk = t[hh * CROSS_HEAD_DIM:(hh + 1) * CROSS_HEAD_DIM, :]
        kn = blk * _rms_scale_cols(blk) * kg_ref[0]
        k_ref[0, 0, hh] = kn.T.astype(BF16)
        v_ref[0, 0, hh] = t[cw + hh * CROSS_HEAD_DIM:cw + (hh + 1) * CROSS_HEAD_DIM, :].astype(BF16)


def _mem_kv(mem, gains, wts, kgains):
    B, M, D = mem.shape
    depth = gains.shape[0]
    kshape = (depth, B, CROSS_HEADS, M, CROSS_HEAD_DIM)
    vshape = (depth, B, CROSS_HEADS, CROSS_HEAD_DIM, M)
    return pl.pallas_call(
        _mem_kv_kernel,
        grid=(depth, B),
        in_specs=[pl.BlockSpec((1, M, D), lambda l, b: (b, 0, 0)),
                  pl.BlockSpec((1, 1, D), lambda l, b: (l, 0, 0)),
                  pl.BlockSpec((1,) + wts.shape[1:], lambda l, b: (l, 0, 0)),
                  pl.BlockSpec((1,) + kgains.shape[1:], lambda l, b: (l, 0, 0))],
        out_specs=[pl.BlockSpec((1, 1) + kshape[2:], lambda l, b: (l, b, 0, 0, 0)),
                   pl.BlockSpec((1, 1) + vshape[2:], lambda l, b: (l, b, 0, 0, 0))],
        out_shape=[jax.ShapeDtypeStruct(kshape, BF16), jax.ShapeDtypeStruct(vshape, BF16)],
        compiler_params=_params(2),
        name="mem_kv",
    )(mem, gains, wts, kgains)


def _post_kernel(*refs, n_attn):
    x_ref = refs[0]
    attn_refs = refs[1:1 + n_attn]
    wout_refs = refs[1 + n_attn:1 + 2 * n_attn]
    g_ref, wq_ref, k_ref, v_ref, wo_ref, o_ref = refs[1 + 2 * n_attn:]
    x = x_ref[...]
    for a_ref, w_ref in zip(attn_refs, wout_refs):
        a = a_ref[...].astype(F32).T.astype(BF16)
        x = x + jnp.dot(a, w_ref[...], preferred_element_type=F32)
    h = _rms_rows(x, g_ref[...]).astype(BF16)
    qt = lax.dot_general(wq_ref[...], h, _NT, preferred_element_type=F32)
    scale = CROSS_HEAD_DIM ** -0.5
    outs = []
    for hh in range(CROSS_HEADS):
        blk = qt[hh * CROSS_HEAD_DIM:(hh + 1) * CROSS_HEAD_DIM, :]
        qn = (blk * (_rms_scale_cols(blk) * scale)).astype(BF16)
        s = jnp.dot(k_ref[0, 0, hh], qn, preferred_element_type=F32)
        p = jnp.exp(s - jnp.max(s, axis=0, keepdims=True))
        l = jnp.sum(p, axis=0, keepdims=True)
        o = jnp.dot(v_ref[0, 0, hh], p.astype(BF16), preferred_element_type=F32)
        outs.append(o / l)
    o_t = jnp.concatenate(outs, axis=0)
    o_ref[...] = x + jnp.dot(o_t.T.astype(BF16), wo_ref[...], preferred_element_type=F32)


def _post_mixer(x, attn_list, wout_list, gain, wq_t, mem_k, mem_v, wo, layer, seq):
    T, D = x.shape
    tile = POST_TILE
    per_seq = seq // tile
    n_attn = len(attn_list)
    const = lambda i: (0, 0)
    in_specs = [pl.BlockSpec((tile, D), lambda i: (i, 0))]
    in_specs += [pl.BlockSpec((a.shape[0], tile), lambda i: (0, i)) for a in attn_list]
    in_specs += [pl.BlockSpec(w.shape, const) for w in wout_list]
    in_specs += [pl.BlockSpec((1, D), const), pl.BlockSpec(wq_t.shape, const),
                 pl.BlockSpec((1, 1) + mem_k.shape[2:], lambda i: (layer, i // per_seq, 0, 0, 0)),
                 pl.BlockSpec((1, 1) + mem_v.shape[2:], lambda i: (layer, i // per_seq, 0, 0, 0)),
                 pl.BlockSpec(wo.shape, const)]
    return pl.pallas_call(
        functools.partial(_post_kernel, n_attn=n_attn),
        grid=(T // tile,),
        in_specs=in_specs,
        out_specs=pl.BlockSpec((tile, D), lambda i: (i, 0)),
        out_shape=jax.ShapeDtypeStruct((T, D), F32),
        compiler_params=_params(1),
        name="outproj_cross",
    )(x, *attn_list, *wout_list, gain, wq_t, mem_k, mem_v, wo)


def _ffn_kernel(x_ref, g_ref, w1_ref, cw_ref, w2_ref, o_ref, tail_ref, ext_ref, *, per_seq):
    i = pl.program_id(0)
    tile = x_ref.shape[0]
    n_chunks = w2_ref.shape[0]
    lead = 8
    x = x_ref[...]
    h = _rms_rows(x, g_ref[...]).astype(BF16)

    @pl.when(i % per_seq == 0)
    def _():
        tail_ref[...] = jnp.zeros(tail_ref.shape, F32)

    o_ref[...] = x

    def conv(c):
        u = jnp.dot(h, w1_ref[c], preferred_element_type=F32)
        ext_ref[0:lead, :] = tail_ref[c]
        ext_ref[lead:, :] = u
        tail_ref[c] = u[tile - lead:, :]
        cw = cw_ref[c]
        return (cw[3:4, :] + cw[0:1, :] * ext_ref[lead - 2:lead - 2 + tile, :]
                + cw[1:2, :] * ext_ref[lead - 1:lead - 1 + tile, :] + cw[2:3, :] * u)

    def chunk(c, _):
        gate = conv(c)
        up = conv(c + n_chunks)
        act = gate * (1.0 / (1.0 + jnp.exp(-gate))) * up
        o_ref[...] += jnp.dot(act.astype(BF16), w2_ref[c], preferred_element_type=F32)
        return 0

    lax.fori_loop(0, n_chunks, chunk, 0)


def _conv_ffn(x, gain, w1, cw, w2, seq):
    T, D = x.shape
    tile = FFN_TILE
    const3 = lambda i: (0, 0, 0)
    return pl.pallas_call(
        functools.partial(_ffn_kernel, per_seq=seq // tile),
        grid=(T // tile,),
        in_specs=[pl.BlockSpec((tile, D), lambda i: (i, 0)),
                  pl.BlockSpec((1, D), lambda i: (0, 0)),
                  pl.BlockSpec(w1.shape, const3, pipeline_mode=pl.Buffered(1)),
                  pl.BlockSpec(cw.shape, const3),
                  pl.BlockSpec(w2.shape, const3, pipeline_mode=pl.Buffered(1))],
        out_specs=pl.BlockSpec((tile, D), lambda i: (i, 0)),
        out_shape=jax.ShapeDtypeStruct((T, D), F32),
        scratch_shapes=[pltpu.VMEM((w1.shape[0], 8, FFN_CHUNK), F32),
                        pltpu.VMEM((tile + 8, FFN_CHUNK), F32)],
        compiler_params=_params(1),
        name="conv_ffn",
    )(x, gain, w1, cw, w2)


def _t5_bucket(dist):
    n = jnp.maximum(dist, 0)
    max_exact = NUM_BUCKETS // 2
    nf = jnp.maximum(n, 1).astype(F32)
    coef = (NUM_BUCKETS - max_exact) / math.log(MAX_DISTANCE / max_exact)
    large = max_exact + (jnp.log(nf / max_exact) * coef).astype(jnp.int32)
    large = jnp.minimum(large, NUM_BUCKETS - 1)
    return jnp.where(n < max_exact, n, large)


def _toeplitz(windows, rows, cols):
    period = windows.shape[-1]
    assert period >= rows + cols + 1
    lead = windows.shape[:-1]
    reps = jnp.tile(windows, (1,) * len(lead) + (rows,))[..., :rows * (period - 1)]
    skew = reps.reshape(lead + (rows, period - 1))
    return skew[..., rows:rows + cols]


def _moba_bias_tiles(bias_by_dist):
    n_tiles = min(bias_by_dist.shape[1] // TB, -(-(MAX_DISTANCE + TB) // TB))
    period = 3 * TB
    padded = jnp.pad(bias_by_dist, ((0, 0), (TB, period)))
    windows = jnp.stack([padded[:, d * TB:d * TB + period] for d in range(n_tiles)], axis=1)
    tiles = _toeplitz(windows, TB, TB)
    key = np.arange(TB)[:, None]
    qry = np.arange(TB)[None, :]
    causal = jnp.asarray((qry >= key)[None, None] | (np.arange(n_tiles) > 0)[None, :, None, None])
    return jnp.where(causal, tiles, NEG_INF)


def _swa_bias_tiles(bias_by_dist):
    rows = SWA_WINDOW + TB
    period = rows + TB + SWA_WINDOW
    padded = jnp.pad(bias_by_dist[:, :period], ((0, 0), (TB, 0)))[:, :period]
    tiles = _toeplitz(padded, rows, TB)
    dist = np.arange(TB)[None, :] - np.arange(rows)[:, None] + SWA_WINDOW
    in_window = jnp.asarray(((dist >= 0) & (dist < SWA_WINDOW))[None])
    return jnp.where(in_window, tiles, NEG_INF)


def kernel(x, mem, rel_bias, mix_norm, ev_w_in, ev_w_out, ev_q_gain, ev_k_gain, od_w_in, od_w_out, od_q_gain, od_k_gain, od_sinks, cx_norm, cx_mem_norm, cx_w_q, cx_w_kv, cx_w_o, cx_q_gain, cx_k_gain, ff_norm, ff_w_in, ff_conv_w, ff_conv_b, ff_w_out):
    B, S, D = x.shape
    T = B * S
    depth = mix_norm.shape[0]
    n_heads = rel_bias.shape[1]
    sb_heads = n_heads // 2
    d_ff = ff_w_out.shape[1]
    n_chunks = d_ff // FFN_CHUNK
    n_kv = (od_w_in.shape[2] // HEAD_DIM - n_heads) // 2
    assert S % FFN_TILE == 0 and S % POST_TILE == 0 and S % TB == 0 and d_ff % FFN_CHUNK == 0

    def lane_bcast(col, width):
        return jnp.broadcast_to(col[:, None], (col.shape[0], width)).astype(F32)

    bias_by_dist = rel_bias.astype(F32)[_t5_bucket(jnp.arange(S + TB, dtype=jnp.int32))].T
    moba_tiles = _moba_bias_tiles(bias_by_dist[sb_heads:, :S])
    swa_tiles = _swa_bias_tiles(bias_by_dist)

    cx_kgain = jnp.stack([lane_bcast(cx_k_gain[l] * cx_q_gain[l], mem.shape[1]) for l in range(depth)])
    mem_k, mem_v = _mem_kv(mem, cx_mem_norm[:, None, :], jnp.swapaxes(cx_w_kv, 1, 2).astype(BF16),
                           cx_kgain)

    xf = x.reshape(T, D)
    for layer in range(depth):
        idx = layer // 2
        gain = mix_norm[layer][None, :]
        if layer % 2 == 0:
            kgain = lane_bcast(jnp.tile(ev_k_gain[idx] * ev_q_gain[idx], sb_heads), TB)
            qa, ka, va, qb, kb, vb = _even_inproj(xf, gain, ev_w_in[idx].T.astype(BF16), kgain)
            attn = [_sb_attention(qa, ka, va, B), _moba_attention(qb, kb, vb, moba_tiles, B)]
            half = sb_heads * HEAD_DIM
            wout = [ev_w_out[idx, :half].astype(BF16), ev_w_out[idx, half:].astype(BF16)]
        else:
            kgain = lane_bcast(jnp.tile(od_k_gain[idx] * od_q_gain[idx], n_kv), TB)
            q, k, v = _odd_inproj(xf, gain, od_w_in[idx].T.astype(BF16), kgain, n_heads, n_kv)
            attn = [_swa_attention(q, k, v, swa_tiles, od_sinks[idx].astype(F32), B)]
            wout = [od_w_out[idx].astype(BF16)]
        xf = _post_mixer(xf, attn, wout, cx_norm[layer][None, :], cx_w_q[layer].T.astype(BF16),
                         mem_k, mem_v, cx_w_o[layer].astype(BF16), layer, S)
        w1 = ff_w_in[layer].reshape(D, 2 * n_chunks, FFN_CHUNK).transpose(1, 0, 2).astype(BF16)
        cw = jnp.concatenate([ff_conv_w[layer], ff_conv_b[layer][None, :]], axis=0)
        cw = cw.reshape(CONV_WIDTH + 1, 2 * n_chunks, FFN_CHUNK).transpose(1, 0, 2)
        w2 = ff_w_out[layer].reshape(n_chunks, FFN_CHUNK, D).astype(BF16)
        xf = _conv_ffn(xf, ff_norm[layer][None, :], w1, cw, w2, S)
    return xf.reshape(B, S, D)
```

```python
import functools
import math

import jax
import jax.numpy as jnp
import numpy as np
from jax import lax
from jax.experimental import pallas as pl
from jax.experimental.pallas import tpu as pltpu

F32 = jnp.float32
BF16 = jnp.bfloat16

HEAD_DIM = 64
PAIR = 2 * HEAD_DIM
TB = 256
SWA_WINDOW = 128
MOBA_TOPK = 3
MOBA_UNROLL = 4
LOG2E = math.log2(math.e)
NUM_BUCKETS = 32
MAX_DISTANCE = 4096
CROSS_HEADS = 4
CROSS_HEAD_DIM = 128
CONV_WIDTH = 3
RMS_EPS = 1e-6
NEG_INF = -1e30
SB_DEAD_LOG2 = -150.5
SB_HEADS_PER_STEP = 4
FFN_CHUNK = 256
FFN_TILE = 512
POST_TILE = 512
VMEM_LIMIT = 56 * 1024 * 1024

_NT = (((1,), (1,)), ((), ()))
_TN = (((0,), (0,)), ((), ()))


def _rms_rows(x, gain):
    ms = jnp.mean(x * x, axis=-1, keepdims=True)
    return (x * lax.rsqrt(ms + RMS_EPS)) * gain


def _rms_scale_cols(xt):
    return lax.rsqrt(jnp.mean(xt * xt, axis=0, keepdims=True) + RMS_EPS)


def _params(n_axes):
    return pltpu.CompilerParams(dimension_semantics=("arbitrary",) * n_axes,
                                vmem_limit_bytes=VMEM_LIMIT)


def _pair_extend(q, head):
    odd = (head % 2).astype(F32)
    q32 = q.astype(F32)
    return jnp.concatenate([q32 * (1.0 - odd), q32 * odd], axis=0).astype(BF16)


def _even_inproj_kernel(x_ref, g_ref, wt_ref, kg_ref,
                        qa_ref, ka_ref, va_ref, qb_ref, kb_ref, vb_ref):
    h = _rms_rows(x_ref[...], g_ref[...]).astype(BF16)
    width = 8 * HEAD_DIM

    def proj(group):
        w = wt_ref[group * width:(group + 1) * width, :]
        return lax.dot_general(w, h, _NT, preferred_element_type=F32)

    def heads(t):
        return [t[hh * HEAD_DIM:(hh + 1) * HEAD_DIM, :] for hh in range(8)]

    scale = HEAD_DIM ** -0.5
    for hh, blk in enumerate(heads(proj(0))):
        qa_ref[hh, 0] = (blk * (scale * LOG2E)).astype(BF16)
    ka_ref[...] = proj(1).T.astype(BF16)
    for hh, blk in enumerate(heads(proj(2))):
        va_ref[hh, 0] = blk.astype(BF16)
    for hh, blk in enumerate(heads(proj(3))):
        qb_ref[hh, 0] = (blk * (_rms_scale_cols(blk) * (scale * LOG2E))).astype(BF16)
    kb = jnp.concatenate([blk * _rms_scale_cols(blk) for blk in heads(proj(4))], axis=0)
    kb_ref[...] = (kb * kg_ref[...]).T.astype(BF16)
    for hh, blk in enumerate(heads(proj(5))):
        vb_ref[hh, 0] = blk.astype(BF16)


def _even_inproj(x, gain, wt, kgain):
    T, D = x.shape
    nblk = T // TB
    blocked = jax.ShapeDtypeStruct((8, nblk, HEAD_DIM, TB), BF16)
    rowmajor = jax.ShapeDtypeStruct((T, 8 * HEAD_DIM), BF16)
    bspec = pl.BlockSpec((8, 1, HEAD_DIM, TB), lambda i: (0, i, 0, 0))
    rspec = pl.BlockSpec((TB, 8 * HEAD_DIM), lambda i: (i, 0))
    return pl.pallas_call(
        _even_inproj_kernel,
        grid=(nblk,),
        in_specs=[pl.BlockSpec((TB, D), lambda i: (i, 0)),
                  pl.BlockSpec((1, D), lambda i: (0, 0)),
                  pl.BlockSpec(wt.shape, lambda i: (0, 0)),
                  pl.BlockSpec(kgain.shape, lambda i: (0, 0))],
        out_specs=[bspec, rspec, bspec, bspec, rspec, bspec],
        out_shape=[blocked, rowmajor, blocked, blocked, rowmajor, blocked],
        compiler_params=_params(1),
        name="even_inproj",
    )(x, gain, wt, kgain)


def _odd_inproj_kernel(x_ref, g_ref, wt_ref, kg_ref, q_ref, k_ref, v_ref, *, n_q, n_kv):
    h = _rms_rows(x_ref[...], g_ref[...]).astype(BF16)
    scale = HEAD_DIM ** -0.5 * LOG2E
    group = 8 * HEAD_DIM
    for g in range(n_q * HEAD_DIM // group):
        w = wt_ref[g * group:(g + 1) * group, :]
        t = lax.dot_general(w, h, _NT, preferred_element_type=F32)
        for hh in range(8):
            blk = t[hh * HEAD_DIM:(hh + 1) * HEAD_DIM, :]
            q_ref[g * 8 + hh, 0] = (blk * (_rms_scale_cols(blk) * scale)).astype(BF16)
    kv_rows = n_kv * HEAD_DIM
    w = wt_ref[n_q * HEAD_DIM:n_q * HEAD_DIM + 2 * kv_rows, :]
    t = lax.dot_general(w, h, _NT, preferred_element_type=F32)
    kparts = []
    for hh in range(n_kv):
        blk = t[hh * HEAD_DIM:(hh + 1) * HEAD_DIM, :]
        kparts.append(blk * _rms_scale_cols(blk))
        v_ref[hh, 0] = t[kv_rows + hh * HEAD_DIM:kv_rows + (hh + 1) * HEAD_DIM, :].astype(BF16)
    k_ref[...] = (jnp.concatenate(kparts, axis=0) * kg_ref[...]).T.astype(BF16)


def _odd_inproj(x, gain, wt, kgain, n_q, n_kv):
    T, D = x.shape
    nblk = T // TB
    return pl.pallas_call(
        functools.partial(_odd_inproj_kernel, n_q=n_q, n_kv=n_kv),
        grid=(nblk,),
        in_specs=[pl.BlockSpec((TB, D), lambda i: (i, 0)),
                  pl.BlockSpec((1, D), lambda i: (0, 0)),
                  pl.BlockSpec(wt.shape, lambda i: (0, 0)),
                  pl.BlockSpec(kgain.shape, lambda i: (0, 0))],
        out_specs=[pl.BlockSpec((n_q, 1, HEAD_DIM, TB), lambda i: (0, i, 0, 0)),
                   pl.BlockSpec((TB, n_kv * HEAD_DIM), lambda i: (i, 0)),
                   pl.BlockSpec((n_kv, 1, HEAD_DIM, TB), lambda i: (0, i, 0, 0))],
        out_shape=[jax.ShapeDtypeStruct((n_q, nblk, HEAD_DIM, TB), BF16),
                   jax.ShapeDtypeStruct((T, n_kv * HEAD_DIM), BF16),
                   jax.ShapeDtypeStruct((n_kv, nblk, HEAD_DIM, TB), BF16)],
        compiler_params=_params(1),
        name="odd_inproj",
    )(x, gain, wt, kgain)


def _sb_kernel(q_ref, k_ref, v_ref, o_ref, carry_ref, acc_ref, lb_ref, hl_ref, w_ref):
    i = pl.program_id(2)
    n_heads = q_ref.shape[0]
    key_idx = lax.broadcasted_iota(jnp.int32, (TB, TB), 0)
    qry_idx = lax.broadcasted_iota(jnp.int32, (TB, TB), 1)
    later = jnp.where(qry_idx > key_idx, 1.0, 0.0).astype(BF16)
    later2 = jnp.concatenate([later, later], axis=1)
    past = key_idx < qry_idx
    zeros = jnp.zeros((HEAD_DIM, TB), BF16)

    def group(j, diagonal):
        rows = pl.ds(pl.multiple_of(j * TB, TB), TB)
        colsums = []
        for hh in range(n_heads):
            q = q_ref[hh, 0]
            q_ext = jnp.concatenate([q, zeros] if hh % 2 == 0 else [zeros, q], axis=0)
            kj = k_ref[rows, (hh // 2) * PAIR:(hh // 2 + 1) * PAIR]
            z = jnp.dot(kj, q_ext, preferred_element_type=F32)
            log_beta = jnp.minimum(z, 0.0) - jnp.log2(1.0 + jnp.exp2(-jnp.abs(z)))
            log_not = log_beta - z
            if diagonal:
                log_not = jnp.where(past, log_not, 0.0)
            hi = log_not.astype(BF16)
            lb_ref[hh] = log_beta
            hl_ref[hh, :TB, :] = hi
            hl_ref[hh, TB:, :] = (log_not - hi.astype(F32)).astype(BF16)
            colsums.append(jnp.sum(log_not, axis=0, keepdims=True))
        carries = []
        for hh in range(n_heads):
            between = jnp.dot(later2, hl_ref[hh], preferred_element_type=F32)
            carry = 0.0 if diagonal else carry_ref[hh]
            w = jnp.exp2(lb_ref[hh] + between + carry)
            if diagonal:
                w = jnp.where(past, w, 0.0)
            w_ref[hh] = w.astype(BF16)
            carries.append(carry + colsums[hh])
            carry_ref[hh] = carries[hh]
        for hh in range(n_heads):
            pv = jnp.dot(v_ref[hh, j], w_ref[hh], preferred_element_type=F32)
            acc_ref[hh] = pv if diagonal else acc_ref[hh] + pv
        return carries

    def alive_flag(carries):
        top = functools.reduce(jnp.maximum, carries)
        return (jnp.max(top) > SB_DEAD_LOG2).astype(jnp.int32)

    alive = alive_flag(group(i, True))

    def cond(state):
        return jnp.logical_and(state[0] >= 0, state[1] > 0)

    def body(state):
        j = state[0]
        return j - 1, alive_flag(group(j, False))

    lax.while_loop(cond, body, (i - 1, alive))
    for hh in range(n_heads):
        o_ref[hh * HEAD_DIM:(hh + 1) * HEAD_DIM, :] = acc_ref[hh].astype(BF16)


def _sb_attention(qt, k, vt, batch):
    n_heads, nblk = qt.shape[0], qt.shape[1]
    nq = nblk // batch
    S = nq * TB
    step = SB_HEADS_PER_STEP
    assert n_heads % step == 0 and step % 2 == 0
    return pl.pallas_call(
        _sb_kernel,
        grid=(batch, n_heads // step, nq),
        in_specs=[pl.BlockSpec((step, 1, HEAD_DIM, TB), lambda b, h, i: (h, b * nq + i, 0, 0)),
                  pl.BlockSpec((S, step * HEAD_DIM), lambda b, h, i: (b, h)),
                  pl.BlockSpec((step, nq, HEAD_DIM, TB), lambda b, h, i: (h, b, 0, 0))],
        out_specs=pl.BlockSpec((step * HEAD_DIM, TB), lambda b, h, i: (h, b * nq + i)),
        out_shape=jax.ShapeDtypeStruct((n_heads * HEAD_DIM, nblk * TB), BF16),
        scratch_shapes=[pltpu.VMEM((step, 1, TB), F32), pltpu.VMEM((step, HEAD_DIM, TB), F32),
                        pltpu.VMEM((step, TB, TB), F32), pltpu.VMEM((step, 2 * TB, TB), BF16),
                        pltpu.VMEM((step, TB, TB), BF16)],
        compiler_params=_params(3),
        name="stickbreak_attn",
    )(qt, k, vt)


def _moba_kernel(q_ref, k_ref, v_ref, bias_ref, o_ref, kmean_ref, sel_ref, s_ref, *, n_tiles):
    head = pl.program_id(0)
    i = pl.program_id(2)
    nb = kmean_ref.shape[0]
    q_ext = _pair_extend(q_ref[0, 0], head)

    @pl.when(i == 0)
    def _():
        def one(jb, _):
            blk = k_ref[pl.ds(pl.multiple_of(jb * TB, TB), TB), :].astype(F32)
            kmean_ref[pl.ds(jb, 1), :] = jnp.mean(blk, axis=0, keepdims=True)
            return 0
        lax.fori_loop(0, nb, one, 0)

    km = kmean_ref[...]
    km_hi = km.astype(BF16)
    km_lo = (km - km_hi.astype(F32)).astype(BF16)
    gate = (jnp.dot(km_hi, q_ext, preferred_element_type=F32)
            + jnp.dot(km_lo, q_ext, preferred_element_type=F32))
    blk_idx = lax.broadcasted_iota(jnp.int32, (nb, TB), 0)
    gate = jnp.where(blk_idx < i, gate, NEG_INF)
    sel = jnp.where(blk_idx == i, 1.0, 0.0)
    for _ in range(MOBA_TOPK):
        best = jnp.max(gate, axis=0, keepdims=True)
        cand = jnp.where(jnp.logical_and(gate == best, best > 0.5 * NEG_INF), blk_idx, nb)
        pick = blk_idx == jnp.min(cand, axis=0, keepdims=True)
        sel = jnp.where(pick, 1.0, sel)
        gate = jnp.where(pick, NEG_INF, gate)
    sel_ref[...] = sel

    n_groups = (i + MOBA_UNROLL) // MOBA_UNROLL

    def score_group(g, m):
        for u in range(MOBA_UNROLL):
            j = g * MOBA_UNROLL + u
            kj = k_ref[pl.ds(pl.multiple_of(j * TB, TB), TB), :]
            tile = jnp.clip(i - j, 0, n_tiles - 1)
            s = jnp.dot(kj, q_ext, preferred_element_type=F32) + bias_ref[0, tile]
            s = jnp.where(sel_ref[pl.ds(j, 1), :] > 0.0, s, NEG_INF)
            s_ref[j] = s
            m = jnp.maximum(m, jnp.max(s, axis=0, keepdims=True))
        return m

    m = lax.fori_loop(0, n_groups, score_group, jnp.full((1, TB), NEG_INF, F32))

    def value_group(g, state):
        l, acc = state
        for u in range(MOBA_UNROLL):
            j = g * MOBA_UNROLL + u
            p = jnp.exp2(s_ref[j] - m)
            l = l + jnp.sum(p, axis=0, keepdims=True)
            acc = acc + jnp.dot(v_ref[0, j], p.astype(BF16), preferred_element_type=F32)
        return l, acc

    l, acc = lax.fori_loop(0, n_groups, value_group,
                           (jnp.zeros((1, TB), F32), jnp.zeros((HEAD_DIM, TB), F32)))
    o_ref[...] = (acc / l).astype(BF16)


def _moba_attention(qt, k, vt, bias_tiles, batch):
    n_heads, nblk = qt.shape[0], qt.shape[1]
    nq = nblk // batch
    S = nq * TB
    n_tiles = bias_tiles.shape[1]
    return pl.pallas_call(
        functools.partial(_moba_kernel, n_tiles=n_tiles),
        grid=(n_heads, batch, nq),
        in_specs=[pl.BlockSpec((1, 1, HEAD_DIM, TB), lambda h, b, i: (h, b * nq + i, 0, 0)),
                  pl.BlockSpec((S, PAIR), lambda h, b, i: (b, h // 2)),
                  pl.BlockSpec((1, nq, HEAD_DIM, TB), lambda h, b, i: (h, b, 0, 0)),
                  pl.BlockSpec((1, n_tiles, TB, TB), lambda h, b, i: (h, 0, 0, 0))],
        out_specs=pl.BlockSpec((HEAD_DIM, TB), lambda h, b, i: (h, b * nq + i)),
        out_shape=jax.ShapeDtypeStruct((n_heads * HEAD_DIM, nblk * TB), BF16),
        scratch_shapes=[pltpu.VMEM((nq, PAIR), F32), pltpu.VMEM((nq, TB), F32),
                        pltpu.VMEM((nq, TB, TB), F32)],
        compiler_params=_params(3),
        name="moba_attn",
    )(qt, k, vt, bias_tiles)


def _swa_kernel(sink_ref, q_ref, kp_ref, kc_ref, vp_ref, vc_ref, bias_ref, o_ref, s_ref, p_ref):
    kv_head = pl.program_id(0)
    i = pl.program_id(2)
    group = q_ref.shape[0]
    w = SWA_WINDOW
    kp, kc = kp_ref[...], kc_ref[...]
    vp, vc = vp_ref[0, 0], vc_ref[0, 0]
    keys = [jnp.concatenate([kp[TB - w:, :], kc[:w, :]], axis=0), kc]
    vals = [jnp.concatenate([vp[:, TB - w:], vc[:, :w]], axis=1), vc]
    first_rows_exist = jnp.logical_or(
        lax.broadcasted_iota(jnp.int32, (2 * w, w), 0) >= w, i > 0)
    n_halves = TB // w
    sinks = [sink_ref[kv_head * group + hh] * LOG2E for hh in range(group)]
    maxes, denoms = [], []
    for hh in range(group):
        q_ext = _pair_extend(q_ref[hh, 0], kv_head)
        for u in range(n_halves):
            s = (jnp.dot(keys[u], q_ext[:, u * w:(u + 1) * w], preferred_element_type=F32)
                 + bias_ref[hh])
            if u == 0:
                s = jnp.where(first_rows_exist, s, NEG_INF)
            s_ref[hh * n_halves + u] = s
            maxes.append(jnp.maximum(jnp.max(s, axis=0, keepdims=True), sinks[hh]))
    for c in range(group * n_halves):
        p = jnp.exp2(s_ref[c] - maxes[c])
        denoms.append(jnp.sum(p, axis=0, keepdims=True) + jnp.exp2(sinks[c // n_halves] - maxes[c]))
        p_ref[c] = p.astype(BF16)
    for hh in range(group):
        halves = [jnp.dot(vals[u], p_ref[hh * n_halves + u], preferred_element_type=F32)
                  / denoms[hh * n_halves + u] for u in range(n_halves)]
        o_ref[hh * HEAD_DIM:(hh + 1) * HEAD_DIM, :] = jnp.concatenate(halves, axis=1).astype(BF16)


def _swa_attention(qt, k, vt, bias_tiles, sinks, batch):
    n_heads, nblk = qt.shape[0], qt.shape[1]
    n_kv = vt.shape[0]
    group = n_heads // n_kv
    nq = nblk // batch
    assert TB == 2 * SWA_WINDOW

    def prev(b, i):
        return b * nq + jnp.maximum(i - 1, 0)

    return pl.pallas_call(
        _swa_kernel,
        grid_spec=pltpu.PrefetchScalarGridSpec(
            num_scalar_prefetch=1,
            grid=(n_kv, batch, nq),
            in_specs=[
                pl.BlockSpec((group, 1, HEAD_DIM, TB), lambda g, b, i, s: (g, b * nq + i, 0, 0)),
                pl.BlockSpec((TB, PAIR), lambda g, b, i, s: (prev(b, i), g // 2)),
                pl.BlockSpec((TB, PAIR), lambda g, b, i, s: (b * nq + i, g // 2)),
                pl.BlockSpec((1, 1, HEAD_DIM, TB), lambda g, b, i, s: (g, prev(b, i), 0, 0)),
                pl.BlockSpec((1, 1, HEAD_DIM, TB), lambda g, b, i, s: (g, b * nq + i, 0, 0)),
                pl.BlockSpec((group, 2 * SWA_WINDOW, SWA_WINDOW), lambda g, b, i, s: (g, 0, 0)),
            ],
            out_specs=pl.BlockSpec((group * HEAD_DIM, TB), lambda g, b, i, s: (g, b * nq + i)),
            scratch_shapes=[pltpu.VMEM((2 * group, 2 * SWA_WINDOW, SWA_WINDOW), F32),
                            pltpu.VMEM((2 * group, 2 * SWA_WINDOW, SWA_WINDOW), BF16)],
        ),
        out_shape=jax.ShapeDtypeStruct((n_heads * HEAD_DIM, nblk * TB), BF16),
        compiler_params=_params(3),
        name="swa_attn",
    )(sinks, qt, k, k, vt, vt, bias_tiles)


def _mem_kv_kernel(mem_ref, g_ref, wt_ref, kg_ref, k_ref, v_ref):
    h = _rms_rows(mem_ref[0], g_ref[0]).astype(BF16)
    t = lax.dot_general(wt_ref[0], h, _NT, preferred_element_type=F32)
    cw = CROSS_HEADS * CROSS_HEAD_DIM
    for hh in range(CROSS_HEADS):
        blk = t[hh * CROSS_HEAD_DIM:(hh + 1) * CROSS_HEAD_DIM, :]
        kn = blk * _rms_scale_cols(blk) * kg_ref[0]
        k_ref[0, 0, hh] = kn.T.astype(BF16)
        v_ref[0, 0, hh] = t[cw + hh * CROSS_HEAD_DIM:cw + (hh + 1) * CROSS_HEAD_DIM, :].astype(BF16)


def _mem_kv(mem, gains, wts, kgains):
    B, M, D = mem.shape
    depth = gains.shape[0]
    kshape = (depth, B, CROSS_HEADS, M, CROSS_HEAD_DIM)
    vshape = (depth, B, CROSS_HEADS, CROSS_HEAD_DIM, M)
    return pl.pallas_call(
        _mem_kv_kernel,
        grid=(depth, B),
        in_specs=[pl.BlockSpec((1, M, D), lambda l, b: (b, 0, 0)),
                  pl.BlockSpec((1, 1, D), lambda l, b: (l, 0, 0)),
                  pl.BlockSpec((1,) + wts.shape[1:], lambda l, b: (l, 0, 0)),
                  pl.BlockSpec((1,) + kgains.shape[1:], lambda l, b: (l, 0, 0))],
        out_specs=[pl.BlockSpec((1, 1) + kshape[2:], lambda l, b: (l, b, 0, 0, 0)),
                   pl.BlockSpec((1, 1) + vshape[2:], lambda l, b: (l, b, 0, 0, 0))],
        out_shape=[jax.ShapeDtypeStruct(kshape, BF16), jax.ShapeDtypeStruct(vshape, BF16)],
        compiler_params=_params(2),
        name="mem_kv",
    )(mem, gains, wts, kgains)


def _post_kernel(*refs, n_attn):
    x_ref = refs[0]
    attn_refs = refs[1:1 + n_attn]
    wout_refs = refs[1 + n_attn:1 + 2 * n_attn]
    g_ref, wq_ref, k_ref, v_ref, wo_ref, o_ref = refs[1 + 2 * n_attn:]
    x = x_ref[...]
    for a_ref, w_ref in zip(attn_refs, wout_refs):
        a = a_ref[...].astype(F32).T.astype(BF16)
        x = x + jnp.dot(a, w_ref[...], preferred_element_type=F32)
    h = _rms_rows(x, g_ref[...]).astype(BF16)
    qt = lax.dot_general(wq_ref[...], h, _NT, preferred_element_type=F32)
    scale = CROSS_HEAD_DIM ** -0.5
    outs = []
    for hh in range(CROSS_HEADS):
        blk = qt[hh * CROSS_HEAD_DIM:(hh + 1) * CROSS_HEAD_DIM, :]
        qn = (blk * (_rms_scale_cols(blk) * scale)).astype(BF16)
        s = jnp.dot(k_ref[0, 0, hh], qn, preferred_element_type=F32)
        p = jnp.exp(s - jnp.max(s, axis=0, keepdims=True))
        l = jnp.sum(p, axis=0, keepdims=True)
        o = jnp.dot(v_ref[0, 0, hh], p.astype(BF16), preferred_element_type=F32)
        outs.append(o / l)
    o_t = jnp.concatenate(outs, axis=0)
    o_ref[...] = x + jnp.dot(o_t.T.astype(BF16), wo_ref[...], preferred_element_type=F32)


def _post_mixer(x, attn_list, wout_list, gain, wq_t, mem_k, mem_v, wo, layer, seq):
    T, D = x.shape
    tile = POST_TILE
    per_seq = seq // tile
    n_attn = len(attn_list)
    const = lambda i: (0, 0)
    in_specs = [pl.BlockSpec((tile, D), lambda i: (i, 0))]
    in_specs += [pl.BlockSpec((a.shape[0], tile), lambda i: (0, i)) for a in attn_list]
    in_specs += [pl.BlockSpec(w.shape, const) for w in wout_list]
    in_specs += [pl.BlockSpec((1, D), const), pl.BlockSpec(wq_t.shape, const),
                 pl.BlockSpec((1, 1) + mem_k.shape[2:], lambda i: (layer, i // per_seq, 0, 0, 0)),
                 pl.BlockSpec((1, 1) + mem_v.shape[2:], lambda i: (layer, i // per_seq, 0, 0, 0)),
                 pl.BlockSpec(wo.shape, const)]
    return pl.pallas_call(
        functools.partial(_post_kernel, n_attn=n_attn),
        grid=(T // tile,),
        in_specs=in_specs,
        out_specs=pl.BlockSpec((tile, D), lambda i: (i, 0)),
        out_shape=jax.ShapeDtypeStruct((T, D), F32),
        compiler_params=_params(1),
        name="outproj_cross",
    )(x, *attn_list, *wout_list, gain, wq_t, mem_k, mem_v, wo)


def _ffn_kernel(x_ref, g_ref, w1_ref, cw_ref, w2_ref, o_ref, tail_ref, ext_ref, *, per_seq):
    i = pl.program_id(0)
    tile = x_ref.shape[0]
    n_chunks = w2_ref.shape[0]
    lead = 8
    x = x_ref[...]
    h = _rms_rows(x, g_ref[...]).astype(BF16)

    @pl.when(i % per_seq == 0)
    def _():
        tail_ref[...] = jnp.zeros(tail_ref.shape, F32)

    o_ref[...] = x

    def conv(c):
        u = jnp.dot(h, w1_ref[c], preferred_element_type=F32)
        ext_ref[0:lead, :] = tail_ref[c]
        ext_ref[lead:, :] = u
        tail_ref[c] = u[tile - lead:, :]
        cw = cw_ref[c]
        return (cw[3:4, :] + cw[0:1, :] * ext_ref[lead - 2:lead - 2 + tile, :]
                + cw[1:2, :] * ext_ref[lead - 1:lead - 1 + tile, :] + cw[2:3, :] * u)

    def chunk(c, _):
        gate = conv(c)
        up = conv(c + n_chunks)
        act = gate * (1.0 / (1.0 + jnp.exp(-gate))) * up
        o_ref[...] += jnp.dot(act.astype(BF16), w2_ref[c], preferred_element_type=F32)
        return 0

    lax.fori_loop(0, n_chunks, chunk, 0)


def _conv_ffn(x, gain, w1, cw, w2, seq):
    T, D = x.shape
    tile = FFN_TILE
    const3 = lambda i: (0, 0, 0)
    return pl.pallas_call(
        functools.partial(_ffn_kernel, per_seq=seq // tile),
        grid=(T // tile,),
        in_specs=[pl.BlockSpec((tile, D), lambda i: (i, 0)),
                  pl.BlockSpec((1, D), lambda i: (0, 0)),
                  pl.BlockSpec(w1.shape, const3, pipeline_mode=pl.Buffered(1)),
                  pl.BlockSpec(cw.shape, const3),
                  pl.BlockSpec(w2.shape, const3, pipeline_mode=pl.Buffered(1))],
        out_specs=pl.BlockSpec((tile, D), lambda i: (i, 0)),
        out_shape=jax.ShapeDtypeStruct((T, D), F32),
        scratch_shapes=[pltpu.VMEM((w1.shape[0], 8, FFN_CHUNK), F32),
                        pltpu.VMEM((tile + 8, FFN_CHUNK), F32)],
        compiler_params=_params(1),
        name="conv_ffn",
    )(x, gain, w1, cw, w2)


def _t5_bucket(dist):
    n = jnp.maximum(dist, 0)
    max_exact = NUM_BUCKETS // 2
    nf = jnp.maximum(n, 1).astype(F32)
    coef = (NUM_BUCKETS - max_exact) / math.log(MAX_DISTANCE / max_exact)
    large = max_exact + (jnp.log(nf / max_exact) * coef).astype(jnp.int32)
    large = jnp.minimum(large, NUM_BUCKETS - 1)
    return jnp.where(n < max_exact, n, large)


def _toeplitz(windows, rows, cols):
    period = windows.shape[-1]
    assert period >= rows + cols + 1
    lead = windows.shape[:-1]
    reps = jnp.tile(windows, (1,) * len(lead) + (rows,))[..., :rows * (period - 1)]
    skew = reps.reshape(lead + (rows, period - 1))
    return skew[..., rows:rows + cols]


def _moba_bias_tiles(bias_by_dist):
    n_tiles = min(bias_by_dist.shape[1] // TB, -(-(MAX_DISTANCE + TB) // TB))
    period = 3 * TB
    padded = jnp.pad(bias_by_dist, ((0, 0), (TB, period)))
    windows = jnp.stack([padded[:, d * TB:d * TB + period] for d in range(n_tiles)], axis=1)
    tiles = _toeplitz(windows, TB, TB)
    key = np.arange(TB)[:, None]
    qry = np.arange(TB)[None, :]
    causal = jnp.asarray((qry >= key)[None, None] | (np.arange(n_tiles) > 0)[None, :, None, None])
    return jnp.where(causal, tiles, NEG_INF)


def _swa_bias_tiles(bias_by_dist):
    w = SWA_WINDOW
    rows = 2 * w
    period = rows + w + w
    tiles = _toeplitz(jnp.pad(bias_by_dist[:, :period - w], ((0, 0), (w, 0))), rows, w)
    dist = np.arange(w)[None, :] - np.arange(rows)[:, None] + w
    in_window = jnp.asarray(((dist >= 0) & (dist < w))[None])
    return jnp.where(in_window, tiles, NEG_INF)


def kernel(x, mem, rel_bias, mix_norm, ev_w_in, ev_w_out, ev_q_gain, ev_k_gain, od_w_in, od_w_out, od_q_gain, od_k_gain, od_sinks, cx_norm, cx_mem_norm, cx_w_q, cx_w_kv, cx_w_o, cx_q_gain, cx_k_gain, ff_norm, ff_w_in, ff_conv_w, ff_conv_b, ff_w_out):
    B, S, D = x.shape
    T = B * S
    depth = mix_norm.shape[0]
    n_heads = rel_bias.shape[1]
    sb_heads = n_heads // 2
    d_ff = ff_w_out.shape[1]
    n_chunks = d_ff // FFN_CHUNK
    n_kv = (od_w_in.shape[2] // HEAD_DIM - n_heads) // 2
    assert S % FFN_TILE == 0 and S % POST_TILE == 0 and S % TB == 0 and d_ff % FFN_CHUNK == 0

    def lane_bcast(col, width):
        return jnp.broadcast_to(col[:, None], (col.shape[0], width)).astype(F32)

    bias_by_dist = rel_bias.astype(F32)[_t5_bucket(jnp.arange(S + TB, dtype=jnp.int32))].T
    assert (S // TB) % MOBA_UNROLL == 0
    moba_tiles = _moba_bias_tiles(bias_by_dist[sb_heads:, :S] * LOG2E)
    swa_tiles = _swa_bias_tiles(bias_by_dist * LOG2E)

    cx_kgain = jnp.stack([lane_bcast(cx_k_gain[l] * cx_q_gain[l], mem.shape[1]) for l in range(depth)])
    mem_k, mem_v = _mem_kv(mem, cx_mem_norm[:, None, :], jnp.swapaxes(cx_w_kv, 1, 2).astype(BF16),
                           cx_kgain)

    xf = x.reshape(T, D)
    for layer in range(depth):
        idx = layer // 2
        gain = mix_norm[layer][None, :]
        if layer % 2 == 0:
            kgain = lane_bcast(jnp.tile(ev_k_gain[idx] * ev_q_gain[idx], sb_heads), TB)
            qa, ka, va, qb, kb, vb = _even_inproj(xf, gain, ev_w_in[idx].T.astype(BF16), kgain)
            attn = [_sb_attention(qa, ka, va, B), _moba_attention(qb, kb, vb, moba_tiles, B)]
            half = sb_heads * HEAD_DIM
            wout = [ev_w_out[idx, :half].astype(BF16), ev_w_out[idx, half:].astype(BF16)]
        else:
            kgain = lane_bcast(jnp.tile(od_k_gain[idx] * od_q_gain[idx], n_kv), TB)
            q, k, v = _odd_inproj(xf, gain, od_w_in[idx].T.astype(BF16), kgain, n_heads, n_kv)
            attn = [_swa_attention(q, k, v, swa_tiles, od_sinks[idx].astype(F32), B)]
            wout = [od_w_out[idx].astype(BF16)]
        xf = _post_mixer(xf, attn, wout, cx_norm[layer][None, :], cx_w_q[layer].T.astype(BF16),
                         mem_k, mem_v, cx_w_o[layer].astype(BF16), layer, S)
        w1 = ff_w_in[layer].reshape(D, 2 * n_chunks, FFN_CHUNK).transpose(1, 0, 2).astype(BF16)
        cw = jnp.concatenate([ff_conv_w[layer], ff_conv_b[layer][None, :]], axis=0)
        cw = cw.reshape(CONV_WIDTH + 1, 2 * n_chunks, FFN_CHUNK).transpose(1, 0, 2)
        w2 = ff_w_out[layer].reshape(n_chunks, FFN_CHUNK, D).astype(BF16)
        xf = _conv_ffn(xf, ff_norm[layer][None, :], w1, cw, w2, S)
    return xf.reshape(B, S, D)
```

```python
import functools
import math

import jax
import jax.numpy as jnp
import numpy as np
from jax import lax
from jax.experimental import pallas as pl
from jax.experimental.pallas import tpu as pltpu

F32 = jnp.float32
BF16 = jnp.bfloat16

HEAD_DIM = 64
PAIR = 2 * HEAD_DIM
TB = 256
SWA_WINDOW = 128
MOBA_TOPK = 3
MOBA_UNROLL = 4
MOBA_HEADS_PER_STEP = 2
LOG2E = math.log2(math.e)
NUM_BUCKETS = 32
MAX_DISTANCE = 4096
CROSS_HEADS = 4
CROSS_HEAD_DIM = 128
CONV_WIDTH = 3
RMS_EPS = 1e-6
NEG_INF = -1e30
SB_DEAD_LOG2 = -150.5
SB_HEADS_PER_STEP = 4
FFN_CHUNK = 256
FFN_TILE = 512
POST_TILE = 512
VMEM_LIMIT = 56 * 1024 * 1024

_NT = (((1,), (1,)), ((), ()))
_TN = (((0,), (0,)), ((), ()))


def _rms_rows(x, gain):
    ms = jnp.mean(x * x, axis=-1, keepdims=True)
    return (x * lax.rsqrt(ms + RMS_EPS)) * gain


def _rms_scale_cols(xt):
    return lax.rsqrt(jnp.mean(xt * xt, axis=0, keepdims=True) + RMS_EPS)


def _params(n_axes):
    return pltpu.CompilerParams(dimension_semantics=("arbitrary",) * n_axes,
                                vmem_limit_bytes=VMEM_LIMIT)


def _pair_extend(q, head):
    odd = (head % 2).astype(F32)
    q32 = q.astype(F32)
    return jnp.concatenate([q32 * (1.0 - odd), q32 * odd], axis=0).astype(BF16)


def _even_inproj_kernel(x_ref, g_ref, wt_ref, kg_ref,
                        qa_ref, ka_ref, va_ref, qb_ref, kb_ref, vb_ref):
    h = _rms_rows(x_ref[...], g_ref[...]).astype(BF16)
    width = 8 * HEAD_DIM

    def proj(group):
        w = wt_ref[group * width:(group + 1) * width, :]
        return lax.dot_general(w, h, _NT, preferred_element_type=F32)

    def heads(t):
        return [t[hh * HEAD_DIM:(hh + 1) * HEAD_DIM, :] for hh in range(8)]

    scale = HEAD_DIM ** -0.5
    for hh, blk in enumerate(heads(proj(0))):
        qa_ref[hh, 0] = (blk * (scale * LOG2E)).astype(BF16)
    ka_ref[...] = proj(1).T.astype(BF16)
    for hh, blk in enumerate(heads(proj(2))):
        va_ref[hh, 0] = blk.astype(BF16)
    for hh, blk in enumerate(heads(proj(3))):
        qb_ref[hh, 0] = (blk * (_rms_scale_cols(blk) * (scale * LOG2E))).astype(BF16)
    one_then_zeros = jnp.where(lax.broadcasted_iota(jnp.int32, (HEAD_DIM, TB), 0) == 0, 1.0, 0.0)
    kg = kg_ref[...]
    kb = []
    for hh, blk in enumerate(heads(proj(4))):
        kb += [blk * _rms_scale_cols(blk) * kg[hh * HEAD_DIM:(hh + 1) * HEAD_DIM, :], one_then_zeros]
    kb_ref[...] = jnp.concatenate(kb, axis=0).T.astype(BF16)
    for hh, blk in enumerate(heads(proj(5))):
        vb_ref[hh, 0] = blk.astype(BF16)


def _even_inproj(x, gain, wt, kgain):
    T, D = x.shape
    nblk = T // TB
    blocked = jax.ShapeDtypeStruct((8, nblk, HEAD_DIM, TB), BF16)
    rowmajor = jax.ShapeDtypeStruct((T, 8 * HEAD_DIM), BF16)
    bspec = pl.BlockSpec((8, 1, HEAD_DIM, TB), lambda i: (0, i, 0, 0))
    rspec = pl.BlockSpec((TB, 8 * HEAD_DIM), lambda i: (i, 0))
    return pl.pallas_call(
        _even_inproj_kernel,
        grid=(nblk,),
        in_specs=[pl.BlockSpec((TB, D), lambda i: (i, 0)),
                  pl.BlockSpec((1, D), lambda i: (0, 0)),
                  pl.BlockSpec(wt.shape, lambda i: (0, 0)),
                  pl.BlockSpec(kgain.shape, lambda i: (0, 0))],
        out_specs=[bspec, rspec, bspec, bspec, pl.BlockSpec((TB, 8 * PAIR), lambda i: (i, 0)), bspec],
        out_shape=[blocked, rowmajor, blocked, blocked,
                   jax.ShapeDtypeStruct((T, 8 * PAIR), BF16), blocked],
        compiler_params=_params(1),
        name="even_inproj",
    )(x, gain, wt, kgain)


def _odd_inproj_kernel(x_ref, g_ref, wt_ref, kg_ref, q_ref, k_ref, v_ref, *, n_q, n_kv):
    h = _rms_rows(x_ref[...], g_ref[...]).astype(BF16)
    scale = HEAD_DIM ** -0.5 * LOG2E
    group = 8 * HEAD_DIM
    for g in range(n_q * HEAD_DIM // group):
        w = wt_ref[g * group:(g + 1) * group, :]
        t = lax.dot_general(w, h, _NT, preferred_element_type=F32)
        for hh in range(8):
            blk = t[hh * HEAD_DIM:(hh + 1) * HEAD_DIM, :]
            q_ref[g * 8 + hh, 0] = (blk * (_rms_scale_cols(blk) * scale)).astype(BF16)
    kv_rows = n_kv * HEAD_DIM
    w = wt_ref[n_q * HEAD_DIM:n_q * HEAD_DIM + 2 * kv_rows, :]
    t = lax.dot_general(w, h, _NT, preferred_element_type=F32)
    kparts = []
    for hh in range(n_kv):
        blk = t[hh * HEAD_DIM:(hh + 1) * HEAD_DIM, :]
        kparts.append(blk * _rms_scale_cols(blk))
        v_ref[hh, 0] = t[kv_rows + hh * HEAD_DIM:kv_rows + (hh + 1) * HEAD_DIM, :].astype(BF16)
    k_ref[...] = (jnp.concatenate(kparts, axis=0) * kg_ref[...]).T.astype(BF16)


def _odd_inproj(x, gain, wt, kgain, n_q, n_kv):
    T, D = x.shape
    nblk = T // TB
    return pl.pallas_call(
        functools.partial(_odd_inproj_kernel, n_q=n_q, n_kv=n_kv),
        grid=(nblk,),
        in_specs=[pl.BlockSpec((TB, D), lambda i: (i, 0)),
                  pl.BlockSpec((1, D), lambda i: (0, 0)),
                  pl.BlockSpec(wt.shape, lambda i: (0, 0)),
                  pl.BlockSpec(kgain.shape, lambda i: (0, 0))],
        out_specs=[pl.BlockSpec((n_q, 1, HEAD_DIM, TB), lambda i: (0, i, 0, 0)),
                   pl.BlockSpec((TB, n_kv * HEAD_DIM), lambda i: (i, 0)),
                   pl.BlockSpec((n_kv, 1, HEAD_DIM, TB), lambda i: (0, i, 0, 0))],
        out_shape=[jax.ShapeDtypeStruct((n_q, nblk, HEAD_DIM, TB), BF16),
                   jax.ShapeDtypeStruct((T, n_kv * HEAD_DIM), BF16),
                   jax.ShapeDtypeStruct((n_kv, nblk, HEAD_DIM, TB), BF16)],
        compiler_params=_params(1),
        name="odd_inproj",
    )(x, gain, wt, kgain)


def _sb_kernel(q_ref, k_ref, v_ref, o_ref, carry_ref, acc_ref, lb_ref, hl_ref, w_ref):
    i = pl.program_id(2)
    n_heads = q_ref.shape[0]
    key_idx = lax.broadcasted_iota(jnp.int32, (TB, TB), 0)
    qry_idx = lax.broadcasted_iota(jnp.int32, (TB, TB), 1)
    later = jnp.where(qry_idx > key_idx, 1.0, 0.0).astype(BF16)
    later2 = jnp.concatenate([later, later], axis=1)
    past = key_idx < qry_idx
    zeros = jnp.zeros((HEAD_DIM, TB), BF16)

    def group(j, diagonal):
        rows = pl.ds(pl.multiple_of(j * TB, TB), TB)
        colsums = []
        for hh in range(n_heads):
            q = q_ref[hh, 0]
            q_ext = jnp.concatenate([q, zeros] if hh % 2 == 0 else [zeros, q], axis=0)
            kj = k_ref[rows, (hh // 2) * PAIR:(hh // 2 + 1) * PAIR]
            z = jnp.dot(kj, q_ext, preferred_element_type=F32)
            log_beta = jnp.minimum(z, 0.0) - jnp.log2(1.0 + jnp.exp2(-jnp.abs(z)))
            log_not = log_beta - z
            if diagonal:
                log_not = jnp.where(past, log_not, 0.0)
            hi = log_not.astype(BF16)
            lb_ref[hh] = log_beta
            hl_ref[hh, :TB, :] = hi
            hl_ref[hh, TB:, :] = (log_not - hi.astype(F32)).astype(BF16)
            colsums.append(jnp.sum(log_not, axis=0, keepdims=True))
        carries = []
        for hh in range(n_heads):
            between = jnp.dot(later2, hl_ref[hh], preferred_element_type=F32)
            carry = 0.0 if diagonal else carry_ref[hh]
            w = jnp.exp2(lb_ref[hh] + between + carry)
            if diagonal:
                w = jnp.where(past, w, 0.0)
            w_ref[hh] = w.astype(BF16)
            carries.append(carry + colsums[hh])
            carry_ref[hh] = carries[hh]
        for hh in range(n_heads):
            pv = jnp.dot(v_ref[hh, j], w_ref[hh], preferred_element_type=F32)
            acc_ref[hh] = pv if diagonal else acc_ref[hh] + pv
        return carries

    def alive_flag(carries):
        top = functools.reduce(jnp.maximum, carries)
        return (jnp.max(top) > SB_DEAD_LOG2).astype(jnp.int32)

    alive = alive_flag(group(i, True))

    def cond(state):
        return jnp.logical_and(state[0] >= 0, state[1] > 0)

    def body(state):
        j = state[0]
        return j - 1, alive_flag(group(j, False))

    lax.while_loop(cond, body, (i - 1, alive))
    for hh in range(n_heads):
        o_ref[hh * HEAD_DIM:(hh + 1) * HEAD_DIM, :] = acc_ref[hh].astype(BF16)


def _sb_attention(qt, k, vt, batch):
    n_heads, nblk = qt.shape[0], qt.shape[1]
    nq = nblk // batch
    S = nq * TB
    step = SB_HEADS_PER_STEP
    assert n_heads % step == 0 and step % 2 == 0
    return pl.pallas_call(
        _sb_kernel,
        grid=(batch, n_heads // step, nq),
        in_specs=[pl.BlockSpec((step, 1, HEAD_DIM, TB), lambda b, h, i: (h, b * nq + i, 0, 0)),
                  pl.BlockSpec((S, step * HEAD_DIM), lambda b, h, i: (b, h)),
                  pl.BlockSpec((step, nq, HEAD_DIM, TB), lambda b, h, i: (h, b, 0, 0))],
        out_specs=pl.BlockSpec((step * HEAD_DIM, TB), lambda b, h, i: (h, b * nq + i)),
        out_shape=jax.ShapeDtypeStruct((n_heads * HEAD_DIM, nblk * TB), BF16),
        scratch_shapes=[pltpu.VMEM((step, 1, TB), F32), pltpu.VMEM((step, HEAD_DIM, TB), F32),
                        pltpu.VMEM((step, TB, TB), F32), pltpu.VMEM((step, 2 * TB, TB), BF16),
                        pltpu.VMEM((step, TB, TB), BF16)],
        compiler_params=_params(3),
        name="stickbreak_attn",
    )(qt, k, vt)


def _moba_kernel(q_ref, k_ref, v_ref, bias_ref, o_ref, kmean_ref, mask_ref, s_ref, p_ref, *,
                 n_tiles):
    i = pl.program_id(2)
    n_heads, nb = kmean_ref.shape[0], kmean_ref.shape[1]
    heads = range(n_heads)
    qs = [q_ref[hh, 0] for hh in heads]

    def key_block(hh, j):
        return k_ref[pl.ds(pl.multiple_of(j * TB, TB), TB), hh * PAIR:(hh + 1) * PAIR]

    @pl.when(i == 0)
    def _():
        def one(jb, _):
            for hh in heads:
                blk = key_block(hh, jb).astype(F32)
                kmean_ref[hh, pl.ds(jb, 1), :] = jnp.mean(blk, axis=0, keepdims=True)
            return 0
        lax.fori_loop(0, nb, one, 0)

    blk_idx = lax.broadcasted_iota(jnp.int32, (nb, TB), 0)
    for hh in heads:
        q_gate = jnp.concatenate([qs[hh], jnp.zeros((PAIR - HEAD_DIM, TB), BF16)], axis=0)
        km = kmean_ref[hh]
        km_hi = km.astype(BF16)
        km_lo = (km - km_hi.astype(F32)).astype(BF16)
        gate = (jnp.dot(km_hi, q_gate, preferred_element_type=F32)
                + jnp.dot(km_lo, q_gate, preferred_element_type=F32))
        gate = jnp.where(blk_idx < i, gate, NEG_INF)
        chosen = blk_idx == i
        for _ in range(MOBA_TOPK):
            best = jnp.max(gate, axis=0, keepdims=True)
            cand = jnp.where(jnp.logical_and(gate == best, best > 0.5 * NEG_INF), blk_idx, nb)
            pick = blk_idx == jnp.min(cand, axis=0, keepdims=True)
            chosen = jnp.logical_or(chosen, pick)
            gate = jnp.where(pick, NEG_INF, gate)
        mask_ref[hh] = jnp.where(chosen, 0.0, NEG_INF)

    n_groups = (i + MOBA_UNROLL) // MOBA_UNROLL
    first_row = lax.broadcasted_iota(jnp.int32, (16, TB), 0) == 0
    ones_row = jnp.where(first_row, 1.0, 0.0).astype(BF16)
    q_tail = jnp.zeros((PAIR - HEAD_DIM - 16, TB), BF16)

    def qk_group(hh, g):
        raw = []
        for u in range(MOBA_UNROLL):
            j = g * MOBA_UNROLL + u
            mask_rows = jnp.where(first_row, mask_ref[hh, pl.ds(j, 1), :], 0.0).astype(BF16)
            q_ext = jnp.concatenate([qs[hh], mask_rows, q_tail], axis=0)
            raw.append(jnp.dot(key_block(hh, j), q_ext, preferred_element_type=F32))
        return raw

    def score_group(hh, g, raw):
        top = jnp.full((1, TB), NEG_INF, F32)
        for u in range(MOBA_UNROLL):
            tile = jnp.clip(i - (g * MOBA_UNROLL + u), 0, n_tiles - 1)
            s = raw[u] + bias_ref[hh, tile]
            s_ref[hh, u] = s
            top = jnp.maximum(top, jnp.max(s, axis=0, keepdims=True))
        return top

    def value_group(hh, g, top, state):
        m, acc = state
        m_new = jnp.maximum(m, top)
        for u in range(MOBA_UNROLL):
            p_ref[hh, u] = jnp.exp2(s_ref[hh, u] - m_new).astype(BF16)
        pv = None
        for u in range(MOBA_UNROLL):
            v_ext = jnp.concatenate([v_ref[hh, g * MOBA_UNROLL + u], ones_row], axis=0)
            part = jnp.dot(v_ext, p_ref[hh, u], preferred_element_type=F32)
            pv = part if pv is None else pv + part
        return m_new, jnp.exp2(m - m_new) * acc + pv

    def step(g, carry):
        tops, states = carry
        raws = [qk_group(hh, g + 1) for hh in heads]
        states = [value_group(hh, g, tops[hh], states[hh]) for hh in heads]
        return [score_group(hh, g + 1, raws[hh]) for hh in heads], states

    init = [(jnp.full((1, TB), NEG_INF, F32), jnp.zeros((HEAD_DIM + 16, TB), F32))
            for _ in heads]
    raws = [qk_group(hh, 0) for hh in heads]
    tops = [score_group(hh, 0, raws[hh]) for hh in heads]
    tops, states = lax.fori_loop(0, n_groups - 1, step, (tops, init))
    for hh in heads:
        _, acc = value_group(hh, n_groups - 1, tops[hh], states[hh])
        o_ref[hh * HEAD_DIM:(hh + 1) * HEAD_DIM, :] = (
            acc[:HEAD_DIM] / acc[HEAD_DIM:HEAD_DIM + 1]).astype(BF16)


def _moba_attention(qt, k, vt, bias_tiles, batch):
    n_heads, nblk = qt.shape[0], qt.shape[1]
    nq = nblk // batch
    S = nq * TB
    n_tiles = bias_tiles.shape[1]
    step = MOBA_HEADS_PER_STEP
    assert n_heads % step == 0 and nq % MOBA_UNROLL == 0
    return pl.pallas_call(
        functools.partial(_moba_kernel, n_tiles=n_tiles),
        grid=(n_heads // step, batch, nq),
        in_specs=[pl.BlockSpec((step, 1, HEAD_DIM, TB), lambda h, b, i: (h, b * nq + i, 0, 0)),
                  pl.BlockSpec((S, step * PAIR), lambda h, b, i: (b, h)),
                  pl.BlockSpec((step, nq, HEAD_DIM, TB), lambda h, b, i: (h, b, 0, 0)),
                  pl.BlockSpec((step, n_tiles, TB, TB), lambda h, b, i: (h, 0, 0, 0))],
        out_specs=pl.BlockSpec((step * HEAD_DIM, TB), lambda h, b, i: (h, b * nq + i)),
        out_shape=jax.ShapeDtypeStruct((n_heads * HEAD_DIM, nblk * TB), BF16),
        scratch_shapes=[pltpu.VMEM((step, nq, PAIR), F32), pltpu.VMEM((step, nq, TB), F32),
                        pltpu.VMEM((step, MOBA_UNROLL, TB, TB), F32),
                        pltpu.VMEM((step, MOBA_UNROLL, TB, TB), BF16)],
        compiler_params=_params(3),
        name="moba_attn",
    )(qt, k, vt, bias_tiles)


def _swa_kernel(sink_ref, q_ref, kp_ref, kc_ref, vp_ref, vc_ref, bias_ref, o_ref, s_ref, p_ref):
    kv_head = pl.program_id(0)
    i = pl.program_id(2)
    group = q_ref.shape[0]
    w = SWA_WINDOW
    kp, kc = kp_ref[...], kc_ref[...]
    vp, vc = vp_ref[0, 0], vc_ref[0, 0]
    keys = [jnp.concatenate([kp[TB - w:, :], kc[:w, :]], axis=0), kc]
    vals = [jnp.concatenate([vp[:, TB - w:], vc[:, :w]], axis=1), vc]
    first_rows_exist = jnp.logical_or(
        lax.broadcasted_iota(jnp.int32, (2 * w, w), 0) >= w, i > 0)
    n_halves = TB // w
    sinks = [sink_ref[kv_head * group + hh] * LOG2E for hh in range(group)]
    maxes, denoms = [], []
    for hh in range(group):
        q_ext = _pair_extend(q_ref[hh, 0], kv_head)
        for u in range(n_halves):
            s = (jnp.dot(keys[u], q_ext[:, u * w:(u + 1) * w], preferred_element_type=F32)
                 + bias_ref[hh])
            if u == 0:
                s = jnp.where(first_rows_exist, s, NEG_INF)
            s_ref[hh * n_halves + u] = s
            maxes.append(jnp.maximum(jnp.max(s, axis=0, keepdims=True), sinks[hh]))
    for c in range(group * n_halves):
        p = jnp.exp2(s_ref[c] - maxes[c])
        denoms.append(jnp.sum(p, axis=0, keepdims=True) + jnp.exp2(sinks[c // n_halves] - maxes[c]))
        p_ref[c] = p.astype(BF16)
    for hh in range(group):
        halves = [jnp.dot(vals[u], p_ref[hh * n_halves + u], preferred_element_type=F32)
                  / denoms[hh * n_halves + u] for u in range(n_halves)]
        o_ref[hh * HEAD_DIM:(hh + 1) * HEAD_DIM, :] = jnp.concatenate(halves, axis=1).astype(BF16)


def _swa_attention(qt, k, vt, bias_tiles, sinks, batch):
    n_heads, nblk = qt.shape[0], qt.shape[1]
    n_kv = vt.shape[0]
    group = n_heads // n_kv
    nq = nblk // batch
    assert TB == 2 * SWA_WINDOW

    def prev(b, i):
        return b * nq + jnp.maximum(i - 1, 0)

    return pl.pallas_call(
        _swa_kernel,
        grid_spec=pltpu.PrefetchScalarGridSpec(
            num_scalar_prefetch=1,
            grid=(n_kv, batch, nq),
            in_specs=[
                pl.BlockSpec((group, 1, HEAD_DIM, TB), lambda g, b, i, s: (g, b * nq + i, 0, 0)),
                pl.BlockSpec((TB, PAIR), lambda g, b, i, s: (prev(b, i), g // 2)),
                pl.BlockSpec((TB, PAIR), lambda g, b, i, s: (b * nq + i, g // 2)),
                pl.BlockSpec((1, 1, HEAD_DIM, TB), lambda g, b, i, s: (g, prev(b, i), 0, 0)),
                pl.BlockSpec((1, 1, HEAD_DIM, TB), lambda g, b, i, s: (g, b * nq + i, 0, 0)),
                pl.BlockSpec((group, 2 * SWA_WINDOW, SWA_WINDOW), lambda g, b, i, s: (g, 0, 0)),
            ],
            out_specs=pl.BlockSpec((group * HEAD_DIM, TB), lambda g, b, i, s: (g, b * nq + i)),
            scratch_shapes=[pltpu.VMEM((2 * group, 2 * SWA_WINDOW, SWA_WINDOW), F32),
                            pltpu.VMEM((2 * group, 2 * SWA_WINDOW, SWA_WINDOW), BF16)],
        ),
        out_shape=jax.ShapeDtypeStruct((n_heads * HEAD_DIM, nblk * TB), BF16),
        compiler_params=_params(3),
        name="swa_attn",
    )(sinks, qt, k, k, vt, vt, bias_tiles)


def _mem_kv_kernel(mem_ref, g_ref, wt_ref, kg_ref, k_ref, v_ref):
    h = _rms_rows(mem_ref[0], g_ref[0]).astype(BF16)
    t = lax.dot_general(wt_ref[0], h, _NT, preferred_element_type=F32)
    cw = CROSS_HEADS * CROSS_HEAD_DIM
    for hh in range(CROSS_HEADS):
        blk = t[hh * CROSS_HEAD_DIM:(hh + 1) * CROSS_HEAD_DIM, :]
        kn = blk * _rms_scale_cols(blk) * kg_ref[0]
        k_ref[0, 0, hh] = kn.T.astype(BF16)
        v_ref[0, 0, hh] = t[cw + hh * CROSS_HEAD_DIM:cw + (hh + 1) * CROSS_HEAD_DIM, :].astype(BF16)


def _mem_kv(mem, gains, wts, kgains):
    B, M, D = mem.shape
    depth = gains.shape[0]
    kshape = (depth, B, CROSS_HEADS, M, CROSS_HEAD_DIM)
    vshape = (depth, B, CROSS_HEADS, CROSS_HEAD_DIM, M)
    return pl.pallas_call(
        _mem_kv_kernel,
        grid=(depth, B),
        in_specs=[pl.BlockSpec((1, M, D), lambda l, b: (b, 0, 0)),
                  pl.BlockSpec((1, 1, D), lambda l, b: (l, 0, 0)),
                  pl.BlockSpec((1,) + wts.shape[1:], lambda l, b: (l, 0, 0)),
                  pl.BlockSpec((1,) + kgains.shape[1:], lambda l, b: (l, 0, 0))],
        out_specs=[pl.BlockSpec((1, 1) + kshape[2:], lambda l, b: (l, b, 0, 0, 0)),
                   pl.BlockSpec((1, 1) + vshape[2:], lambda l, b: (l, b, 0, 0, 0))],
        out_shape=[jax.ShapeDtypeStruct(kshape, BF16), jax.ShapeDtypeStruct(vshape, BF16)],
        compiler_params=_params(2),
        name="mem_kv",
    )(mem, gains, wts, kgains)


def _post_kernel(*refs, n_attn):
    x_ref = refs[0]
    attn_refs = refs[1:1 + n_attn]
    wout_refs = refs[1 + n_attn:1 + 2 * n_attn]
    g_ref, wq_ref, k_ref, v_ref, wo_ref, o_ref = refs[1 + 2 * n_attn:]
    x = x_ref[...]
    for a_ref, w_ref in zip(attn_refs, wout_refs):
        a = a_ref[...].astype(F32).T.astype(BF16)
        x = x + jnp.dot(a, w_ref[...], preferred_element_type=F32)
    h = _rms_rows(x, g_ref[...]).astype(BF16)
    qt = lax.dot_general(wq_ref[...], h, _NT, preferred_element_type=F32)
    scale = CROSS_HEAD_DIM ** -0.5
    outs = []
    for hh in range(CROSS_HEADS):
        blk = qt[hh * CROSS_HEAD_DIM:(hh + 1) * CROSS_HEAD_DIM, :]
        qn = (blk * (_rms_scale_cols(blk) * scale)).astype(BF16)
        s = jnp.dot(k_ref[0, 0, hh], qn, preferred_element_type=F32)
        p = jnp.exp(s - jnp.max(s, axis=0, keepdims=True))
        l = jnp.sum(p, axis=0, keepdims=True)
        o = jnp.dot(v_ref[0, 0, hh], p.astype(BF16), preferred_element_type=F32)
        outs.append(o / l)
    o_t = jnp.concatenate(outs, axis=0)
    o_ref[...] = x + jnp.dot(o_t.T.astype(BF16), wo_ref[...], preferred_element_type=F32)


def _post_mixer(x, attn_list, wout_list, gain, wq_t, mem_k, mem_v, wo, layer, seq):
    T, D = x.shape
    tile = POST_TILE
    per_seq = seq // tile
    n_attn = len(attn_list)
    const = lambda i: (0, 0)
    in_specs = [pl.BlockSpec((tile, D), lambda i: (i, 0))]
    in_specs += [pl.BlockSpec((a.shape[0], tile), lambda i: (0, i)) for a in attn_list]
    in_specs += [pl.BlockSpec(w.shape, const) for w in wout_list]
    in_specs += [pl.BlockSpec((1, D), const), pl.BlockSpec(wq_t.shape, const),
                 pl.BlockSpec((1, 1) + mem_k.shape[2:], lambda i: (layer, i // per_seq, 0, 0, 0)),
                 pl.BlockSpec((1, 1) + mem_v.shape[2:], lambda i: (layer, i // per_seq, 0, 0, 0)),
                 pl.BlockSpec(wo.shape, const)]
    return pl.pallas_call(
        functools.partial(_post_kernel, n_attn=n_attn),
        grid=(T // tile,),
        in_specs=in_specs,
        out_specs=pl.BlockSpec((tile, D), lambda i: (i, 0)),
        out_shape=jax.ShapeDtypeStruct((T, D), F32),
        compiler_params=_params(1),
        name="outproj_cross",
    )(x, *attn_list, *wout_list, gain, wq_t, mem_k, mem_v, wo)


def _ffn_kernel(x_ref, g_ref, w1_ref, cw_ref, w2_ref, o_ref, tail_ref, ext_ref, h_ref, *, per_seq):
    i = pl.program_id(0)
    tile = x_ref.shape[0]
    n_chunks = w2_ref.shape[0]
    lead = 8
    x = x_ref[...]
    h_ref[...] = _rms_rows(x, g_ref[...]).astype(BF16)

    @pl.when(i % per_seq == 0)
    def _():
        tail_ref[...] = jnp.zeros(tail_ref.shape, F32)

    o_ref[...] = x

    def project(c, slot):
        for part in range(2):
            cc = c + part * n_chunks
            u = jnp.dot(h_ref[...], w1_ref[cc], preferred_element_type=F32)
            ext_ref[slot, part, 0:lead, :] = tail_ref[cc]
            ext_ref[slot, part, lead:, :] = u
            tail_ref[cc] = u[tile - lead:, :]

    def consume(c, slot):
        halves = []
        for part in range(2):
            cw = cw_ref[c + part * n_chunks]
            ext = ext_ref.at[slot, part]
            halves.append(cw[3:4, :] + cw[0:1, :] * ext[lead - 2:lead - 2 + tile, :]
                          + cw[1:2, :] * ext[lead - 1:lead - 1 + tile, :]
                          + cw[2:3, :] * ext[lead:, :])
        gate, up = halves
        act = gate * (1.0 / (1.0 + jnp.exp(-gate))) * up
        o_ref[...] += jnp.dot(act.astype(BF16), w2_ref[c], preferred_element_type=F32)

    project(0, 0)

    def pair(t, _):
        c = 2 * t
        project(c + 1, 1)
        consume(c, 0)
        project(c + 2, 0)
        consume(c + 1, 1)
        return 0

    n_pairs = (n_chunks - 1) // 2
    lax.fori_loop(0, n_pairs, pair, 0)
    if n_chunks % 2 == 0:
        project(n_chunks - 1, 1)
        consume(n_chunks - 2, 0)
        consume(n_chunks - 1, 1)
    else:
        consume(n_chunks - 1, 0)


def _conv_ffn(x, gain, w1, cw, w2, seq):
    T, D = x.shape
    tile = FFN_TILE
    const3 = lambda i: (0, 0, 0)
    return pl.pallas_call(
        functools.partial(_ffn_kernel, per_seq=seq // tile),
        grid=(T // tile,),
        in_specs=[pl.BlockSpec((tile, D), lambda i: (i, 0)),
                  pl.BlockSpec((1, D), lambda i: (0, 0)),
                  pl.BlockSpec(w1.shape, const3, pipeline_mode=pl.Buffered(1)),
                  pl.BlockSpec(cw.shape, const3),
                  pl.BlockSpec(w2.shape, const3, pipeline_mode=pl.Buffered(1))],
        out_specs=pl.BlockSpec((tile, D), lambda i: (i, 0)),
        out_shape=jax.ShapeDtypeStruct((T, D), F32),
        scratch_shapes=[pltpu.VMEM((w1.shape[0], 8, FFN_CHUNK), F32),
                        pltpu.VMEM((2, 2, tile + 8, FFN_CHUNK), F32),
                        pltpu.VMEM((tile, D), BF16)],
        compiler_params=_params(1),
        name="conv_ffn",
    )(x, gain, w1, cw, w2)


def _t5_bucket(dist):
    n = jnp.maximum(dist, 0)
    max_exact = NUM_BUCKETS // 2
    nf = jnp.maximum(n, 1).astype(F32)
    coef = (NUM_BUCKETS - max_exact) / math.log(MAX_DISTANCE / max_exact)
    large = max_exact + (jnp.log(nf / max_exact) * coef).astype(jnp.int32)
    large = jnp.minimum(large, NUM_BUCKETS - 1)
    return jnp.where(n < max_exact, n, large)


def _toeplitz(windows, rows, cols):
    period = windows.shape[-1]
    assert period >= rows + cols + 1
    lead = windows.shape[:-1]
    reps = jnp.tile(windows, (1,) * len(lead) + (rows,))[..., :rows * (period - 1)]
    skew = reps.reshape(lead + (rows, period - 1))
    return skew[..., rows:rows + cols]


def _moba_bias_tiles(bias_by_dist):
    n_tiles = min(bias_by_dist.shape[1] // TB, -(-(MAX_DISTANCE + TB) // TB))
    period = 3 * TB
    padded = jnp.pad(bias_by_dist, ((0, 0), (TB, period)))
    windows = jnp.stack([padded[:, d * TB:d * TB + period] for d in range(n_tiles)], axis=1)
    tiles = _toeplitz(windows, TB, TB)
    key = np.arange(TB)[:, None]
    qry = np.arange(TB)[None, :]
    causal = jnp.asarray((qry >= key)[None, None] | (np.arange(n_tiles) > 0)[None, :, None, None])
    return jnp.where(causal, tiles, NEG_INF)


def _swa_bias_tiles(bias_by_dist):
    w = SWA_WINDOW
    rows = 2 * w
    period = rows + w + w
    tiles = _toeplitz(jnp.pad(bias_by_dist[:, :period - w], ((0, 0), (w, 0))), rows, w)
    dist = np.arange(w)[None, :] - np.arange(rows)[:, None] + w
    in_window = jnp.asarray(((dist >= 0) & (dist < w))[None])
    return jnp.where(in_window, tiles, NEG_INF)


def kernel(x, mem, rel_bias, mix_norm, ev_w_in, ev_w_out, ev_q_gain, ev_k_gain, od_w_in, od_w_out, od_q_gain, od_k_gain, od_sinks, cx_norm, cx_mem_norm, cx_w_q, cx_w_kv, cx_w_o, cx_q_gain, cx_k_gain, ff_norm, ff_w_in, ff_conv_w, ff_conv_b, ff_w_out):
    B, S, D = x.shape
    T = B * S
    depth = mix_norm.shape[0]
    n_heads = rel_bias.shape[1]
    sb_heads = n_heads // 2
    d_ff = ff_w_out.shape[1]
    n_chunks = d_ff // FFN_CHUNK
    n_kv = (od_w_in.shape[2] // HEAD_DIM - n_heads) // 2
    assert S % FFN_TILE == 0 and S % POST_TILE == 0 and S % TB == 0 and d_ff % FFN_CHUNK == 0

    def lane_bcast(col, width):
        return jnp.broadcast_to(col[:, None], (col.shape[0], width)).astype(F32)

    bias_by_dist = rel_bias.astype(F32)[_t5_bucket(jnp.arange(S + TB, dtype=jnp.int32))].T
    moba_tiles = _moba_bias_tiles(bias_by_dist[sb_heads:, :S] * LOG2E)
    swa_tiles = _swa_bias_tiles(bias_by_dist * LOG2E)

    cx_kgain = jnp.stack([lane_bcast(cx_k_gain[l] * cx_q_gain[l], mem.shape[1]) for l in range(depth)])
    mem_k, mem_v = _mem_kv(mem, cx_mem_norm[:, None, :], jnp.swapaxes(cx_w_kv, 1, 2).astype(BF16),
                           cx_kgain)

    xf = x.reshape(T, D)
    for layer in range(depth):
        idx = layer // 2
        gain = mix_norm[layer][None, :]
        if layer % 2 == 0:
            kgain = lane_bcast(jnp.tile(ev_k_gain[idx] * ev_q_gain[idx], sb_heads), TB)
            qa, ka, va, qb, kb, vb = _even_inproj(xf, gain, ev_w_in[idx].T.astype(BF16), kgain)
            attn = [_sb_attention(qa, ka, va, B), _moba_attention(qb, kb, vb, moba_tiles, B)]
            half = sb_heads * HEAD_DIM
            wout = [ev_w_out[idx, :half].astype(BF16), ev_w_out[idx, half:].astype(BF16)]
        else:
            kgain = lane_bcast(jnp.tile(od_k_gain[idx] * od_q_gain[idx], n_kv), TB)
            q, k, v = _odd_inproj(xf, gain, od_w_in[idx].T.astype(BF16), kgain, n_heads, n_kv)
            attn = [_swa_attention(q, k, v, swa_tiles, od_sinks[idx].astype(F32), B)]
            wout = [od_w_out[idx].astype(BF16)]
        xf = _post_mixer(xf, attn, wout, cx_norm[layer][None, :], cx_w_q[layer].T.astype(BF16),
                         mem_k, mem_v, cx_w_o[layer].astype(BF16), layer, S)
        w1 = ff_w_in[layer].reshape(D, 2 * n_chunks, FFN_CHUNK).transpose(1, 0, 2).astype(BF16)
        cw = jnp.concatenate([ff_conv_w[layer], ff_conv_b[layer][None, :]], axis=0)
        cw = cw.reshape(CONV_WIDTH + 1, 2 * n_chunks, FFN_CHUNK).transpose(1, 0, 2)
        w2 = ff_w_out[layer].reshape(n_chunks, FFN_CHUNK, D).astype(BF16)
        xf = _conv_ffn(xf, ff_norm[layer][None, :], w1, cw, w2, S)
    return xf.reshape(B, S, D)
```

```python
import functools
import math

import jax
import jax.numpy as jnp
import numpy as np
from jax import lax
from jax.experimental import pallas as pl
from jax.experimental.pallas import tpu as pltpu

F32 = jnp.float32
BF16 = jnp.bfloat16

HEAD_DIM = 64
PAIR = 2 * HEAD_DIM
TB = 256
SWA_WINDOW = 128
MOBA_TOPK = 3
MOBA_UNROLL = 4
MOBA_HEADS_PER_STEP = 2
LOG2E = math.log2(math.e)
NUM_BUCKETS = 32
MAX_DISTANCE = 4096
CROSS_HEADS = 4
CROSS_HEAD_DIM = 128
CONV_WIDTH = 3
RMS_EPS = 1e-6
NEG_INF = -1e30
SB_DEAD_LOG2 = -150.5
SB_HEADS_PER_STEP = 4
FFN_CHUNK = 256
FFN_TILE = 512
POST_TILE = 512
VMEM_LIMIT = 56 * 1024 * 1024

_NT = (((1,), (1,)), ((), ()))
_TN = (((0,), (0,)), ((), ()))


def _rms_rows(x, gain):
    ms = jnp.mean(x * x, axis=-1, keepdims=True)
    return (x * lax.rsqrt(ms + RMS_EPS)) * gain


def _rms_scale_cols(xt):
    return lax.rsqrt(jnp.mean(xt * xt, axis=0, keepdims=True) + RMS_EPS)


def _params(n_axes):
    return pltpu.CompilerParams(dimension_semantics=("arbitrary",) * n_axes,
                                vmem_limit_bytes=VMEM_LIMIT)


def _pair_extend(q, head):
    odd = (head % 2).astype(F32)
    q32 = q.astype(F32)
    return jnp.concatenate([q32 * (1.0 - odd), q32 * odd], axis=0).astype(BF16)


def _even_inproj_kernel(x_ref, g_ref, wt_ref, kg_ref,
                        qa_ref, ka_ref, va_ref, qb_ref, kb_ref, vb_ref):
    h = _rms_rows(x_ref[...], g_ref[...]).astype(BF16)
    width = 8 * HEAD_DIM

    def proj(group):
        w = wt_ref[group * width:(group + 1) * width, :]
        return lax.dot_general(w, h, _NT, preferred_element_type=F32)

    def heads(t):
        return [t[hh * HEAD_DIM:(hh + 1) * HEAD_DIM, :] for hh in range(8)]

    scale = HEAD_DIM ** -0.5
    for hh, blk in enumerate(heads(proj(0))):
        qa_ref[hh, 0] = (blk * (scale * LOG2E)).astype(BF16)
    ka_ref[...] = proj(1).T.astype(BF16)
    for hh, blk in enumerate(heads(proj(2))):
        va_ref[hh, 0] = blk.astype(BF16)
    for hh, blk in enumerate(heads(proj(3))):
        qb_ref[hh, 0] = (blk * (_rms_scale_cols(blk) * (scale * LOG2E))).astype(BF16)
    one_then_zeros = jnp.where(lax.broadcasted_iota(jnp.int32, (HEAD_DIM, TB), 0) == 0, 1.0, 0.0)
    kg = kg_ref[...]
    kb = []
    for hh, blk in enumerate(heads(proj(4))):
        kb += [blk * _rms_scale_cols(blk) * kg[hh * HEAD_DIM:(hh + 1) * HEAD_DIM, :], one_then_zeros]
    kb_ref[...] = jnp.concatenate(kb, axis=0).T.astype(BF16)
    for hh, blk in enumerate(heads(proj(5))):
        vb_ref[hh, 0] = blk.astype(BF16)


def _even_inproj(x, gain, wt, kgain):
    T, D = x.shape
    nblk = T // TB
    blocked = jax.ShapeDtypeStruct((8, nblk, HEAD_DIM, TB), BF16)
    rowmajor = jax.ShapeDtypeStruct((T, 8 * HEAD_DIM), BF16)
    bspec = pl.BlockSpec((8, 1, HEAD_DIM, TB), lambda i: (0, i, 0, 0))
    rspec = pl.BlockSpec((TB, 8 * HEAD_DIM), lambda i: (i, 0))
    return pl.pallas_call(
        _even_inproj_kernel,
        grid=(nblk,),
        in_specs=[pl.BlockSpec((TB, D), lambda i: (i, 0)),
                  pl.BlockSpec((1, D), lambda i: (0, 0)),
                  pl.BlockSpec(wt.shape, lambda i: (0, 0)),
                  pl.BlockSpec(kgain.shape, lambda i: (0, 0))],
        out_specs=[bspec, rspec, bspec, bspec, pl.BlockSpec((TB, 8 * PAIR), lambda i: (i, 0)), bspec],
        out_shape=[blocked, rowmajor, blocked, blocked,
                   jax.ShapeDtypeStruct((T, 8 * PAIR), BF16), blocked],
        compiler_params=_params(1),
        name="even_inproj",
    )(x, gain, wt, kgain)


def _odd_inproj_kernel(x_ref, g_ref, wt_ref, kg_ref, q_ref, k_ref, v_ref, *, n_q, n_kv):
    h = _rms_rows(x_ref[...], g_ref[...]).astype(BF16)
    scale = HEAD_DIM ** -0.5 * LOG2E
    group = 8 * HEAD_DIM
    for g in range(n_q * HEAD_DIM // group):
        w = wt_ref[g * group:(g + 1) * group, :]
        t = lax.dot_general(w, h, _NT, preferred_element_type=F32)
        for hh in range(8):
            blk = t[hh * HEAD_DIM:(hh + 1) * HEAD_DIM, :]
            q_ref[g * 8 + hh, 0] = (blk * (_rms_scale_cols(blk) * scale)).astype(BF16)
    kv_rows = n_kv * HEAD_DIM
    w = wt_ref[n_q * HEAD_DIM:n_q * HEAD_DIM + 2 * kv_rows, :]
    t = lax.dot_general(w, h, _NT, preferred_element_type=F32)
    kparts = []
    for hh in range(n_kv):
        blk = t[hh * HEAD_DIM:(hh + 1) * HEAD_DIM, :]
        kparts.append(blk * _rms_scale_cols(blk))
        v_ref[hh, 0] = t[kv_rows + hh * HEAD_DIM:kv_rows + (hh + 1) * HEAD_DIM, :].astype(BF16)
    k_ref[...] = (jnp.concatenate(kparts, axis=0) * kg_ref[...]).T.astype(BF16)


def _odd_inproj(x, gain, wt, kgain, n_q, n_kv):
    T, D = x.shape
    nblk = T // TB
    return pl.pallas_call(
        functools.partial(_odd_inproj_kernel, n_q=n_q, n_kv=n_kv),
        grid=(nblk,),
        in_specs=[pl.BlockSpec((TB, D), lambda i: (i, 0)),
                  pl.BlockSpec((1, D), lambda i: (0, 0)),
                  pl.BlockSpec(wt.shape, lambda i: (0, 0)),
                  pl.BlockSpec(kgain.shape, lambda i: (0, 0))],
        out_specs=[pl.BlockSpec((n_q, 1, HEAD_DIM, TB), lambda i: (0, i, 0, 0)),
                   pl.BlockSpec((TB, n_kv * HEAD_DIM), lambda i: (i, 0)),
                   pl.BlockSpec((n_kv, 1, HEAD_DIM, TB), lambda i: (0, i, 0, 0))],
        out_shape=[jax.ShapeDtypeStruct((n_q, nblk, HEAD_DIM, TB), BF16),
                   jax.ShapeDtypeStruct((T, n_kv * HEAD_DIM), BF16),
                   jax.ShapeDtypeStruct((n_kv, nblk, HEAD_DIM, TB), BF16)],
        compiler_params=_params(1),
        name="odd_inproj",
    )(x, gain, wt, kgain)


def _sb_kernel(q_ref, k_ref, v_ref, o_ref, carry_ref, acc_ref, lb_ref, hl_ref, w_ref):
    i = pl.program_id(2)
    n_heads = q_ref.shape[0]
    key_idx = lax.broadcasted_iota(jnp.int32, (TB, TB), 0)
    qry_idx = lax.broadcasted_iota(jnp.int32, (TB, TB), 1)
    later = jnp.where(qry_idx > key_idx, 1.0, 0.0).astype(BF16)
    later2 = jnp.concatenate([later, later], axis=1)
    past = key_idx < qry_idx
    zeros = jnp.zeros((HEAD_DIM, TB), BF16)

    def group(j, diagonal):
        rows = pl.ds(pl.multiple_of(j * TB, TB), TB)
        colsums = []
        for hh in range(n_heads):
            q = q_ref[hh, 0]
            q_ext = jnp.concatenate([q, zeros] if hh % 2 == 0 else [zeros, q], axis=0)
            kj = k_ref[rows, (hh // 2) * PAIR:(hh // 2 + 1) * PAIR]
            z = jnp.dot(kj, q_ext, preferred_element_type=F32)
            log_beta = jnp.minimum(z, 0.0) - jnp.log2(1.0 + jnp.exp2(-jnp.abs(z)))
            log_not = log_beta - z
            if diagonal:
                log_not = jnp.where(past, log_not, 0.0)
            hi = log_not.astype(BF16)
            lb_ref[hh] = log_beta
            hl_ref[hh, :TB, :] = hi
            hl_ref[hh, TB:, :] = (log_not - hi.astype(F32)).astype(BF16)
            colsums.append(jnp.sum(log_not, axis=0, keepdims=True))
        carries = []
        for hh in range(n_heads):
            between = jnp.dot(later2, hl_ref[hh], preferred_element_type=F32)
            carry = 0.0 if diagonal else carry_ref[hh]
            w = jnp.exp2(lb_ref[hh] + between + carry)
            if diagonal:
                w = jnp.where(past, w, 0.0)
            w_ref[hh] = w.astype(BF16)
            carries.append(carry + colsums[hh])
            carry_ref[hh] = carries[hh]
        for hh in range(n_heads):
            pv = jnp.dot(v_ref[hh, j], w_ref[hh], preferred_element_type=F32)
            acc_ref[hh] = pv if diagonal else acc_ref[hh] + pv
        return carries

    def alive_flag(carries):
        top = functools.reduce(jnp.maximum, carries)
        return (jnp.max(top) > SB_DEAD_LOG2).astype(jnp.int32)

    alive = alive_flag(group(i, True))

    def cond(state):
        return jnp.logical_and(state[0] >= 0, state[1] > 0)

    def body(state):
        j = state[0]
        return j - 1, alive_flag(group(j, False))

    lax.while_loop(cond, body, (i - 1, alive))
    for hh in range(n_heads):
        o_ref[hh * HEAD_DIM:(hh + 1) * HEAD_DIM, :] = acc_ref[hh].astype(BF16)


def _sb_attention(qt, k, vt, batch):
    n_heads, nblk = qt.shape[0], qt.shape[1]
    nq = nblk // batch
    S = nq * TB
    step = SB_HEADS_PER_STEP
    assert n_heads % step == 0 and step % 2 == 0
    return pl.pallas_call(
        _sb_kernel,
        grid=(batch, n_heads // step, nq),
        in_specs=[pl.BlockSpec((step, 1, HEAD_DIM, TB), lambda b, h, i: (h, b * nq + i, 0, 0)),
                  pl.BlockSpec((S, step * HEAD_DIM), lambda b, h, i: (b, h)),
                  pl.BlockSpec((step, nq, HEAD_DIM, TB), lambda b, h, i: (h, b, 0, 0))],
        out_specs=pl.BlockSpec((step * HEAD_DIM, TB), lambda b, h, i: (h, b * nq + i)),
        out_shape=jax.ShapeDtypeStruct((n_heads * HEAD_DIM, nblk * TB), BF16),
        scratch_shapes=[pltpu.VMEM((step, 1, TB), F32), pltpu.VMEM((step, HEAD_DIM, TB), F32),
                        pltpu.VMEM((step, TB, TB), F32), pltpu.VMEM((step, 2 * TB, TB), BF16),
                        pltpu.VMEM((step, TB, TB), BF16)],
        compiler_params=_params(3),
        name="stickbreak_attn",
    )(qt, k, vt)


def _moba_kernel(q_ref, k_ref, v_ref, bias_ref, o_ref, kmean_ref, mask_ref, s_ref, p_ref, *,
                 n_tiles):
    n_heads, nq = q_ref.shape[0], q_ref.shape[1]
    heads = range(n_heads)

    def key_block(hh, j):
        return k_ref[pl.ds(pl.multiple_of(j * TB, TB), TB), hh * PAIR:(hh + 1) * PAIR]

    def block_means(jb, _):
        for hh in heads:
            blk = key_block(hh, jb).astype(F32)
            kmean_ref[hh, pl.ds(jb, 1), :] = jnp.mean(blk, axis=0, keepdims=True)
        return 0

    lax.fori_loop(0, nq, block_means, 0)

    blk_idx = lax.broadcasted_iota(jnp.int32, (nq, TB), 0)

    def choose(i, _):
        for hh in heads:
            q_gate = jnp.concatenate([q_ref[hh, i], jnp.zeros((PAIR - HEAD_DIM, TB), BF16)],
                                     axis=0)
            km = kmean_ref[hh]
            km_hi = km.astype(BF16)
            km_lo = (km - km_hi.astype(F32)).astype(BF16)
            gate = (jnp.dot(km_hi, q_gate, preferred_element_type=F32)
                    + jnp.dot(km_lo, q_gate, preferred_element_type=F32))
            gate = jnp.where(blk_idx < i, gate, NEG_INF)
            chosen = blk_idx == i
            for _ in range(MOBA_TOPK):
                best = jnp.max(gate, axis=0, keepdims=True)
                cand = jnp.where(jnp.logical_and(gate == best, best > 0.5 * NEG_INF),
                                 blk_idx, nq)
                pick = blk_idx == jnp.min(cand, axis=0, keepdims=True)
                chosen = jnp.logical_or(chosen, pick)
                gate = jnp.where(pick, NEG_INF, gate)
            mask_ref[hh, i] = jnp.where(chosen, 0.0, NEG_INF)
        return 0

    lax.fori_loop(0, nq, choose, 0)

    first_row = lax.broadcasted_iota(jnp.int32, (16, TB), 0) == 0
    ones_row = jnp.where(first_row, 1.0, 0.0).astype(BF16)
    q_tail = jnp.zeros((PAIR - HEAD_DIM - 16, TB), BF16)

    def following(item):
        i, g = item
        wrap = g == i // MOBA_UNROLL
        stay = jnp.logical_and(wrap, i == nq - 1)
        return (jnp.where(jnp.logical_and(wrap, jnp.logical_not(stay)), i + 1, i),
                jnp.where(stay, g, jnp.where(wrap, 0, g + 1)))

    def score_group(hh, item, slot):
        i, g = item
        q = q_ref[hh, i]
        raw = []
        for u in range(MOBA_UNROLL):
            j = g * MOBA_UNROLL + u
            mask_rows = jnp.where(first_row, mask_ref[hh, i, pl.ds(j, 1), :], 0.0).astype(BF16)
            q_ext = jnp.concatenate([q, mask_rows, q_tail], axis=0)
            raw.append(jnp.dot(key_block(hh, j), q_ext, preferred_element_type=F32))
        top = jnp.full((1, TB), NEG_INF, F32)
        for u in range(MOBA_UNROLL):
            tile = jnp.clip(i - (g * MOBA_UNROLL + u), 0, n_tiles - 1)
            s = raw[u] + bias_ref[hh, tile]
            s_ref[hh, slot, u] = s
            top = jnp.maximum(top, jnp.max(s, axis=0, keepdims=True))
        return top

    def value_group(hh, item, slot, top, state):
        i, g = item
        fresh = g == 0
        m = jnp.where(fresh, NEG_INF, state[0])
        acc = jnp.where(fresh, 0.0, state[1])
        m_new = jnp.maximum(m, top)
        for u in range(MOBA_UNROLL):
            p_ref[hh, slot, u] = jnp.exp2(s_ref[hh, slot, u] - m_new).astype(BF16)
        pv = None
        for u in range(MOBA_UNROLL):
            v_ext = jnp.concatenate([v_ref[hh, g * MOBA_UNROLL + u], ones_row], axis=0)
            part = jnp.dot(v_ext, p_ref[hh, slot, u], preferred_element_type=F32)
            pv = part if pv is None else pv + part
        acc = jnp.exp2(m - m_new) * acc + pv
        o_ref[hh, i] = (acc[:HEAD_DIM] / acc[HEAD_DIM:HEAD_DIM + 1]).astype(BF16)
        return m_new, acc

    def pair(_, carry):
        item_a, tops_a, states = carry
        item_b = following(item_a)
        item_c = following(item_b)
        tops_b = [score_group(hh, item_b, 1) for hh in heads]
        states = [value_group(hh, item_a, 0, tops_a[hh], states[hh]) for hh in heads]
        tops_c = [score_group(hh, item_c, 0) for hh in heads]
        states = [value_group(hh, item_b, 1, tops_b[hh], states[hh]) for hh in heads]
        return item_c, tops_c, states

    n_groups = nq // MOBA_UNROLL
    n_items = MOBA_UNROLL * n_groups * (n_groups + 1) // 2
    first = (jnp.int32(0), jnp.int32(0))
    init = [(jnp.full((1, TB), NEG_INF, F32), jnp.zeros((HEAD_DIM + 16, TB), F32))
            for _ in heads]
    lax.fori_loop(0, n_items // 2, pair, (first, [score_group(hh, first, 0) for hh in heads], init))


def _moba_attention(qt, k, vt, bias_tiles, batch):
    n_heads, nblk = qt.shape[0], qt.shape[1]
    nq = nblk // batch
    S = nq * TB
    n_tiles = bias_tiles.shape[1]
    step = MOBA_HEADS_PER_STEP
    assert n_heads % step == 0 and nq % MOBA_UNROLL == 0
    return pl.pallas_call(
        functools.partial(_moba_kernel, n_tiles=n_tiles),
        grid=(n_heads // step, batch),
        in_specs=[pl.BlockSpec((step, nq, HEAD_DIM, TB), lambda h, b: (h, b, 0, 0)),
                  pl.BlockSpec((S, step * PAIR), lambda h, b: (b, h)),
                  pl.BlockSpec((step, nq, HEAD_DIM, TB), lambda h, b: (h, b, 0, 0)),
                  pl.BlockSpec((step, n_tiles, TB, TB), lambda h, b: (h, 0, 0, 0),
                               pipeline_mode=pl.Buffered(1))],
        out_specs=pl.BlockSpec((step, nq, HEAD_DIM, TB), lambda h, b: (h, b, 0, 0)),
        out_shape=jax.ShapeDtypeStruct((n_heads, nblk, HEAD_DIM, TB), BF16),
        scratch_shapes=[pltpu.VMEM((step, nq, PAIR), F32), pltpu.VMEM((step, nq, nq, TB), F32),
                        pltpu.VMEM((step, 2, MOBA_UNROLL, TB, TB), F32),
                        pltpu.VMEM((step, 2, MOBA_UNROLL, TB, TB), BF16)],
        compiler_params=_params(2),
        name="moba_attn",
    )(qt, k, vt, bias_tiles)


def _swa_kernel(sink_ref, q_ref, kp_ref, kc_ref, vp_ref, vc_ref, bias_ref, o_ref, s_ref, p_ref):
    kv_head = pl.program_id(0)
    i = pl.program_id(2)
    group = q_ref.shape[0]
    w = SWA_WINDOW
    kp, kc = kp_ref[...], kc_ref[...]
    vp, vc = vp_ref[0, 0], vc_ref[0, 0]
    keys = [jnp.concatenate([kp[TB - w:, :], kc[:w, :]], axis=0), kc]
    vals = [jnp.concatenate([vp[:, TB - w:], vc[:, :w]], axis=1), vc]
    first_rows_exist = jnp.logical_or(
        lax.broadcasted_iota(jnp.int32, (2 * w, w), 0) >= w, i > 0)
    n_halves = TB // w
    sinks = [sink_ref[kv_head * group + hh] * LOG2E for hh in range(group)]
    maxes, denoms = [], []
    for hh in range(group):
        q_ext = _pair_extend(q_ref[hh, 0], kv_head)
        for u in range(n_halves):
            s = (jnp.dot(keys[u], q_ext[:, u * w:(u + 1) * w], preferred_element_type=F32)
                 + bias_ref[hh])
            if u == 0:
                s = jnp.where(first_rows_exist, s, NEG_INF)
            s_ref[hh * n_halves + u] = s
            maxes.append(jnp.maximum(jnp.max(s, axis=0, keepdims=True), sinks[hh]))
    for c in range(group * n_halves):
        p = jnp.exp2(s_ref[c] - maxes[c])
        denoms.append(jnp.sum(p, axis=0, keepdims=True) + jnp.exp2(sinks[c // n_halves] - maxes[c]))
        p_ref[c] = p.astype(BF16)
    for hh in range(group):
        halves = [jnp.dot(vals[u], p_ref[hh * n_halves + u], preferred_element_type=F32)
                  / denoms[hh * n_halves + u] for u in range(n_halves)]
        o_ref[hh * HEAD_DIM:(hh + 1) * HEAD_DIM, :] = jnp.concatenate(halves, axis=1).astype(BF16)


def _swa_attention(qt, k, vt, bias_tiles, sinks, batch):
    n_heads, nblk = qt.shape[0], qt.shape[1]
    n_kv = vt.shape[0]
    group = n_heads // n_kv
    nq = nblk // batch
    assert TB == 2 * SWA_WINDOW

    def prev(b, i):
        return b * nq + jnp.maximum(i - 1, 0)

    return pl.pallas_call(
        _swa_kernel,
        grid_spec=pltpu.PrefetchScalarGridSpec(
            num_scalar_prefetch=1,
            grid=(n_kv, batch, nq),
            in_specs=[
                pl.BlockSpec((group, 1, HEAD_DIM, TB), lambda g, b, i, s: (g, b * nq + i, 0, 0)),
                pl.BlockSpec((TB, PAIR), lambda g, b, i, s: (prev(b, i), g // 2)),
                pl.BlockSpec((TB, PAIR), lambda g, b, i, s: (b * nq + i, g // 2)),
                pl.BlockSpec((1, 1, HEAD_DIM, TB), lambda g, b, i, s: (g, prev(b, i), 0, 0)),
                pl.BlockSpec((1, 1, HEAD_DIM, TB), lambda g, b, i, s: (g, b * nq + i, 0, 0)),
                pl.BlockSpec((group, 2 * SWA_WINDOW, SWA_WINDOW), lambda g, b, i, s: (g, 0, 0)),
            ],
            out_specs=pl.BlockSpec((group * HEAD_DIM, TB), lambda g, b, i, s: (g, b * nq + i)),
            scratch_shapes=[pltpu.VMEM((2 * group, 2 * SWA_WINDOW, SWA_WINDOW), F32),
                            pltpu.VMEM((2 * group, 2 * SWA_WINDOW, SWA_WINDOW), BF16)],
        ),
        out_shape=jax.ShapeDtypeStruct((n_heads * HEAD_DIM, nblk * TB), BF16),
        compiler_params=_params(3),
        name="swa_attn",
    )(sinks, qt, k, k, vt, vt, bias_tiles)


def _mem_kv_kernel(mem_ref, g_ref, wt_ref, kg_ref, k_ref, v_ref):
    h = _rms_rows(mem_ref[0], g_ref[0]).astype(BF16)
    t = lax.dot_general(wt_ref[0], h, _NT, preferred_element_type=F32)
    cw = CROSS_HEADS * CROSS_HEAD_DIM
    for hh in range(CROSS_HEADS):
        blk = t[hh * CROSS_HEAD_DIM:(hh + 1) * CROSS_HEAD_DIM, :]
        kn = blk * _rms_scale_cols(blk) * kg_ref[0]
        k_ref[0, 0, hh] = kn.T.astype(BF16)
        v_ref[0, 0, hh] = t[cw + hh * CROSS_HEAD_DIM:cw + (hh + 1) * CROSS_HEAD_DIM, :].astype(BF16)


def _mem_kv(mem, gains, wts, kgains):
    B, M, D = mem.shape
    depth = gains.shape[0]
    kshape = (depth, B, CROSS_HEADS, M, CROSS_HEAD_DIM)
    vshape = (depth, B, CROSS_HEADS, CROSS_HEAD_DIM, M)
    return pl.pallas_call(
        _mem_kv_kernel,
        grid=(depth, B),
        in_specs=[pl.BlockSpec((1, M, D), lambda l, b: (b, 0, 0)),
                  pl.BlockSpec((1, 1, D), lambda l, b: (l, 0, 0)),
                  pl.BlockSpec((1,) + wts.shape[1:], lambda l, b: (l, 0, 0)),
                  pl.BlockSpec((1,) + kgains.shape[1:], lambda l, b: (l, 0, 0))],
        out_specs=[pl.BlockSpec((1, 1) + kshape[2:], lambda l, b: (l, b, 0, 0, 0)),
                   pl.BlockSpec((1, 1) + vshape[2:], lambda l, b: (l, b, 0, 0, 0))],
        out_shape=[jax.ShapeDtypeStruct(kshape, BF16), jax.ShapeDtypeStruct(vshape, BF16)],
        compiler_params=_params(2),
        name="mem_kv",
    )(mem, gains, wts, kgains)


def _post_kernel(*refs, n_attn):
    x_ref = refs[0]
    attn_refs = refs[1:1 + n_attn]
    wout_refs = refs[1 + n_attn:1 + 2 * n_attn]
    g_ref, wq_ref, k_ref, v_ref, wo_ref, o_ref, h_ref, qn_ref, p_ref, ot_ref = refs[1 + 2 * n_attn:]
    subs = [slice(s * TB, (s + 1) * TB) for s in range(x_ref.shape[0] // TB)]
    hd = CROSS_HEAD_DIM
    for s, rows in enumerate(subs):
        x = x_ref[rows, :]
        for a_ref, w_ref in zip(attn_refs, wout_refs):
            if len(a_ref.shape) == 2:
                a = a_ref[:, rows]
            else:
                a = jnp.concatenate([a_ref[hh, s] for hh in range(a_ref.shape[0])], axis=0)
            a = a.astype(F32).T.astype(BF16)
            x = x + jnp.dot(a, w_ref[...], preferred_element_type=F32)
        o_ref[rows, :] = x
        h_ref[rows, :] = _rms_rows(x, g_ref[...]).astype(BF16)
    scale = hd ** -0.5 * LOG2E
    for cols in subs:
        qt = lax.dot_general(wq_ref[...], h_ref[cols, :], _NT, preferred_element_type=F32)
        for hh in range(CROSS_HEADS):
            blk = qt[hh * hd:(hh + 1) * hd, :]
            qn_ref[hh, :, cols] = (blk * (_rms_scale_cols(blk) * scale)).astype(BF16)
    denoms = {}
    for s, cols in enumerate(subs):
        for hh in range(CROSS_HEADS):
            sc = jnp.dot(k_ref[0, 0, hh], qn_ref[hh, :, cols], preferred_element_type=F32)
            p = jnp.exp2(sc - jnp.max(sc, axis=0, keepdims=True))
            denoms[s, hh] = jnp.sum(p, axis=0, keepdims=True)
            p_ref[hh, :, cols] = p.astype(BF16)
    for s, cols in enumerate(subs):
        for hh in range(CROSS_HEADS):
            o = jnp.dot(v_ref[0, 0, hh], p_ref[hh, :, cols], preferred_element_type=F32)
            ot_ref[hh * hd:(hh + 1) * hd, cols] = o / denoms[s, hh]
    for rows in subs:
        o_ref[rows, :] += jnp.dot(ot_ref[:, rows].T.astype(BF16), wo_ref[...],
                                  preferred_element_type=F32)


def _post_mixer(x, attn_list, wout_list, gain, wq_t, mem_k, mem_v, wo, layer, seq):
    T, D = x.shape
    tile = POST_TILE
    per_seq = seq // tile
    n_attn = len(attn_list)
    const = lambda i: (0, 0)
    in_specs = [pl.BlockSpec((tile, D), lambda i: (i, 0))]
    for a in attn_list:
        if a.ndim == 2:
            in_specs.append(pl.BlockSpec((a.shape[0], tile), lambda i: (0, i)))
        else:
            in_specs.append(pl.BlockSpec((a.shape[0], tile // TB) + a.shape[2:],
                                         lambda i: (0, i, 0, 0)))
    in_specs += [pl.BlockSpec(w.shape, const) for w in wout_list]
    in_specs += [pl.BlockSpec((1, D), const), pl.BlockSpec(wq_t.shape, const),
                 pl.BlockSpec((1, 1) + mem_k.shape[2:], lambda i: (layer, i // per_seq, 0, 0, 0)),
                 pl.BlockSpec((1, 1) + mem_v.shape[2:], lambda i: (layer, i // per_seq, 0, 0, 0)),
                 pl.BlockSpec(wo.shape, const)]
    return pl.pallas_call(
        functools.partial(_post_kernel, n_attn=n_attn),
        grid=(T // tile,),
        in_specs=in_specs,
        out_specs=pl.BlockSpec((tile, D), lambda i: (i, 0)),
        out_shape=jax.ShapeDtypeStruct((T, D), F32),
        scratch_shapes=[pltpu.VMEM((tile, D), BF16),
                        pltpu.VMEM((CROSS_HEADS, CROSS_HEAD_DIM, tile), BF16),
                        pltpu.VMEM((CROSS_HEADS, mem_k.shape[3], tile), BF16),
                        pltpu.VMEM((CROSS_HEADS * CROSS_HEAD_DIM, tile), F32)],
        compiler_params=_params(1),
        name="outproj_cross",
    )(x, *attn_list, *wout_list, gain, wq_t, mem_k, mem_v, wo)


def _ffn_kernel(x_ref, g_ref, w1_ref, cw_ref, w2_ref, o_ref, tail_ref, ext_ref, h_ref, *, per_seq):
    i = pl.program_id(0)
    tile = x_ref.shape[0]
    n_chunks = w2_ref.shape[0]
    lead = 8
    x = x_ref[...]
    h_ref[...] = _rms_rows(x, g_ref[...]).astype(BF16)

    @pl.when(i % per_seq == 0)
    def _():
        tail_ref[...] = jnp.zeros(tail_ref.shape, F32)

    o_ref[...] = x

    def project(c, slot):
        for part in range(2):
            cc = c + part * n_chunks
            u = jnp.dot(h_ref[...], w1_ref[cc], preferred_element_type=F32)
            ext_ref[slot, part, 0:lead, :] = tail_ref[cc]
            ext_ref[slot, part, lead:, :] = u
            tail_ref[cc] = u[tile - lead:, :]

    def consume(c, slot):
        halves = []
        for part in range(2):
            cw = cw_ref[c + part * n_chunks]
            ext = ext_ref.at[slot, part]
            halves.append(cw[3:4, :] + cw[0:1, :] * ext[lead - 2:lead - 2 + tile, :]
                          + cw[1:2, :] * ext[lead - 1:lead - 1 + tile, :]
                          + cw[2:3, :] * ext[lead:, :])
        gate, up = halves
        act = gate * (1.0 / (1.0 + jnp.exp(-gate))) * up
        o_ref[...] += jnp.dot(act.astype(BF16), w2_ref[c], preferred_element_type=F32)

    project(0, 0)

    def pair(t, _):
        c = 2 * t
        project(c + 1, 1)
        consume(c, 0)
        project(c + 2, 0)
        consume(c + 1, 1)
        return 0

    n_pairs = (n_chunks - 1) // 2
    lax.fori_loop(0, n_pairs, pair, 0)
    if n_chunks % 2 == 0:
        project(n_chunks - 1, 1)
        consume(n_chunks - 2, 0)
        consume(n_chunks - 1, 1)
    else:
        consume(n_chunks - 1, 0)


def _conv_ffn(x, gain, w1, cw, w2, seq):
    T, D = x.shape
    tile = FFN_TILE
    const3 = lambda i: (0, 0, 0)
    return pl.pallas_call(
        functools.partial(_ffn_kernel, per_seq=seq // tile),
        grid=(T // tile,),
        in_specs=[pl.BlockSpec((tile, D), lambda i: (i, 0)),
                  pl.BlockSpec((1, D), lambda i: (0, 0)),
                  pl.BlockSpec(w1.shape, const3, pipeline_mode=pl.Buffered(1)),
                  pl.BlockSpec(cw.shape, const3),
                  pl.BlockSpec(w2.shape, const3, pipeline_mode=pl.Buffered(1))],
        out_specs=pl.BlockSpec((tile, D), lambda i: (i, 0)),
        out_shape=jax.ShapeDtypeStruct((T, D), F32),
        scratch_shapes=[pltpu.VMEM((w1.shape[0], 8, FFN_CHUNK), F32),
                        pltpu.VMEM((2, 2, tile + 8, FFN_CHUNK), F32),
                        pltpu.VMEM((tile, D), BF16)],
        compiler_params=_params(1),
        name="conv_ffn",
    )(x, gain, w1, cw, w2)


def _t5_bucket(dist):
    n = jnp.maximum(dist, 0)
    max_exact = NUM_BUCKETS // 2
    nf = jnp.maximum(n, 1).astype(F32)
    coef = (NUM_BUCKETS - max_exact) / math.log(MAX_DISTANCE / max_exact)
    large = max_exact + (jnp.log(nf / max_exact) * coef).astype(jnp.int32)
    large = jnp.minimum(large, NUM_BUCKETS - 1)
    return jnp.where(n < max_exact, n, large)


def _toeplitz_kernel(w_ref, o_ref, *, window):
    rows, cols = o_ref.shape[-2:]
    period = w_ref.shape[-1]
    spread = jnp.broadcast_to(w_ref[0, 0], (rows, period))
    skew = pltpu.roll(spread, 0, 1, stride=1, stride_axis=0)
    key = lax.broadcasted_iota(jnp.int32, (rows, cols), 0)
    qry = lax.broadcasted_iota(jnp.int32, (rows, cols), 1)
    if window is None:
        keep = jnp.logical_or(qry >= key, pl.program_id(1) > 0)
    else:
        keep = jnp.logical_and(qry - key + window >= 0, qry - key < 0)
    o_ref[0, 0] = jnp.where(keep, skew[:, rows:rows + cols], NEG_INF)


def _toeplitz_tiles(windows, rows, cols, window):
    n_heads, n_tiles, _, period = windows.shape
    assert period >= rows + cols and rows % 128 == 0 and cols % 128 == 0 and period % 128 == 0
    return pl.pallas_call(
        functools.partial(_toeplitz_kernel, window=window),
        grid=(n_heads, n_tiles),
        in_specs=[pl.BlockSpec((1, 1, 1, period), lambda h, d: (h, d, 0, 0))],
        out_specs=pl.BlockSpec((1, 1, rows, cols), lambda h, d: (h, d, 0, 0)),
        out_shape=jax.ShapeDtypeStruct((n_heads, n_tiles, rows, cols), F32),
        compiler_params=_params(2),
        name="bias_tiles",
    )(windows)


def _moba_bias_tiles(bias_by_dist):
    n_tiles = min(bias_by_dist.shape[1] // TB, -(-(MAX_DISTANCE + TB) // TB))
    period = 3 * TB
    padded = jnp.pad(bias_by_dist, ((0, 0), (TB, period)))
    windows = jnp.stack([padded[:, d * TB:d * TB + period] for d in range(n_tiles)], axis=1)
    return _toeplitz_tiles(windows[:, :, None, :], TB, TB, None)


def _swa_bias_tiles(bias_by_dist):
    w = SWA_WINDOW
    windows = jnp.pad(bias_by_dist[:, :3 * w], ((0, 0), (w, 0)))
    return _toeplitz_tiles(windows[:, None, None, :], 2 * w, w, w)[:, 0]


def kernel(x, mem, rel_bias, mix_norm, ev_w_in, ev_w_out, ev_q_gain, ev_k_gain, od_w_in, od_w_out, od_q_gain, od_k_gain, od_sinks, cx_norm, cx_mem_norm, cx_w_q, cx_w_kv, cx_w_o, cx_q_gain, cx_k_gain, ff_norm, ff_w_in, ff_conv_w, ff_conv_b, ff_w_out):
    B, S, D = x.shape
    T = B * S
    depth = mix_norm.shape[0]
    n_heads = rel_bias.shape[1]
    sb_heads = n_heads // 2
    d_ff = ff_w_out.shape[1]
    n_chunks = d_ff // FFN_CHUNK
    n_kv = (od_w_in.shape[2] // HEAD_DIM - n_heads) // 2
    assert S % FFN_TILE == 0 and S % POST_TILE == 0 and S % TB == 0 and d_ff % FFN_CHUNK == 0

    def lane_bcast(col, width):
        return jnp.broadcast_to(col[:, None], (col.shape[0], width)).astype(F32)

    bias_by_dist = rel_bias.astype(F32)[_t5_bucket(jnp.arange(S + TB, dtype=jnp.int32))].T
    moba_tiles = _moba_bias_tiles(bias_by_dist[sb_heads:, :S] * LOG2E)
    swa_tiles = _swa_bias_tiles(bias_by_dist * LOG2E)

    cx_kgain = jnp.stack([lane_bcast(cx_k_gain[l] * cx_q_gain[l], mem.shape[1]) for l in range(depth)])
    mem_k, mem_v = _mem_kv(mem, cx_mem_norm[:, None, :], jnp.swapaxes(cx_w_kv, 1, 2).astype(BF16),
                           cx_kgain)

    xf = x.reshape(T, D)
    for layer in range(depth):
        idx = layer // 2
        gain = mix_norm[layer][None, :]
        if layer % 2 == 0:
            kgain = lane_bcast(jnp.tile(ev_k_gain[idx] * ev_q_gain[idx], sb_heads), TB)
            qa, ka, va, qb, kb, vb = _even_inproj(xf, gain, ev_w_in[idx].T.astype(BF16), kgain)
            attn = [_sb_attention(qa, ka, va, B), _moba_attention(qb, kb, vb, moba_tiles, B)]
            half = sb_heads * HEAD_DIM
            wout = [ev_w_out[idx, :half].astype(BF16), ev_w_out[idx, half:].astype(BF16)]
        else:
            kgain = lane_bcast(jnp.tile(od_k_gain[idx] * od_q_gain[idx], n_kv), TB)
            q, k, v = _odd_inproj(xf, gain, od_w_in[idx].T.astype(BF16), kgain, n_heads, n_kv)
            attn = [_swa_attention(q, k, v, swa_tiles, od_sinks[idx].astype(F32), B)]
            wout = [od_w_out[idx].astype(BF16)]
        xf = _post_mixer(xf, attn, wout, cx_norm[layer][None, :], cx_w_q[layer].T.astype(BF16),
                         mem_k, mem_v, cx_w_o[layer].astype(BF16), layer, S)
        w1 = ff_w_in[layer].reshape(D, 2 * n_chunks, FFN_CHUNK).transpose(1, 0, 2).astype(BF16)
        cw = jnp.concatenate([ff_conv_w[layer], ff_conv_b[layer][None, :]], axis=0)
        cw = cw.reshape(CONV_WIDTH + 1, 2 * n_chunks, FFN_CHUNK).transpose(1, 0, 2)
        w2 = ff_w_out[layer].reshape(n_chunks, FFN_CHUNK, D).astype(BF16)
        xf = _conv_ffn(xf, ff_norm[layer][None, :], w1, cw, w2, S)
    return xf.reshape(B, S, D)
```

```python
import functools
import math

import jax
import jax.numpy as jnp
import numpy as np
from jax import lax
from jax.experimental import pallas as pl
from jax.experimental.pallas import tpu as pltpu

F32 = jnp.float32
BF16 = jnp.bfloat16

HEAD_DIM = 64
PAIR = 2 * HEAD_DIM
TB = 256
SWA_WINDOW = 128
MOBA_TOPK = 3
MOBA_UNROLL = 4
MOBA_HEADS_PER_STEP = 2
LOG2E = math.log2(math.e)
NUM_BUCKETS = 32
MAX_DISTANCE = 4096
CROSS_HEADS = 4
CROSS_HEAD_DIM = 128
CONV_WIDTH = 3
RMS_EPS = 1e-6
NEG_INF = -1e30
SB_DEAD_LOG2 = -150.5
SB_HEADS_PER_STEP = 8
FFN_CHUNK = 256
FFN_TILE = 512
FFN_GROUP = 6
POST_TILE = 512
VMEM_LIMIT = 56 * 1024 * 1024

_NT = (((1,), (1,)), ((), ()))
_TN = (((0,), (0,)), ((), ()))


def _rms_rows(x, gain):
    ms = jnp.mean(x * x, axis=-1, keepdims=True)
    return (x * lax.rsqrt(ms + RMS_EPS)) * gain


def _rms_scale_cols(xt):
    return lax.rsqrt(jnp.mean(xt * xt, axis=0, keepdims=True) + RMS_EPS)


def _params(n_axes):
    return pltpu.CompilerParams(dimension_semantics=("arbitrary",) * n_axes,
                                vmem_limit_bytes=VMEM_LIMIT)


def _pair_extend(q, head):
    odd = (head % 2).astype(F32)
    q32 = q.astype(F32)
    return jnp.concatenate([q32 * (1.0 - odd), q32 * odd], axis=0).astype(BF16)


def _even_inproj_kernel(x_ref, g_ref, wt_ref, kg_ref,
                        qa_ref, ka_ref, va_ref, qb_ref, kb_ref, vb_ref):
    h = _rms_rows(x_ref[...], g_ref[...]).astype(BF16)
    width = 8 * HEAD_DIM

    def proj(group):
        w = wt_ref[group * width:(group + 1) * width, :]
        return lax.dot_general(w, h, _NT, preferred_element_type=F32)

    def heads(t):
        return [t[hh * HEAD_DIM:(hh + 1) * HEAD_DIM, :] for hh in range(8)]

    scale = HEAD_DIM ** -0.5
    for hh, blk in enumerate(heads(proj(0))):
        qa_ref[hh, 0] = (blk * (scale * LOG2E)).astype(BF16)
    ka_ref[...] = proj(1).T.astype(BF16)
    for hh, blk in enumerate(heads(proj(2))):
        va_ref[hh, 0] = blk.astype(BF16)
    for hh, blk in enumerate(heads(proj(3))):
        qb_ref[hh, 0] = (blk * (_rms_scale_cols(blk) * (scale * LOG2E))).astype(BF16)
    one_then_zeros = jnp.where(lax.broadcasted_iota(jnp.int32, (HEAD_DIM, TB), 0) == 0, 1.0, 0.0)
    kg = kg_ref[...]
    kb = []
    for hh, blk in enumerate(heads(proj(4))):
        kb += [blk * _rms_scale_cols(blk) * kg[hh * HEAD_DIM:(hh + 1) * HEAD_DIM, :], one_then_zeros]
    kb_ref[...] = jnp.concatenate(kb, axis=0).T.astype(BF16)
    for hh, blk in enumerate(heads(proj(5))):
        vb_ref[hh, 0] = blk.astype(BF16)


def _even_inproj(x, gain, wt, kgain):
    T, D = x.shape
    nblk = T // TB
    blocked = jax.ShapeDtypeStruct((8, nblk, HEAD_DIM, TB), BF16)
    rowmajor = jax.ShapeDtypeStruct((T, 8 * HEAD_DIM), BF16)
    bspec = pl.BlockSpec((8, 1, HEAD_DIM, TB), lambda i: (0, i, 0, 0))
    rspec = pl.BlockSpec((TB, 8 * HEAD_DIM), lambda i: (i, 0))
    return pl.pallas_call(
        _even_inproj_kernel,
        grid=(nblk,),
        in_specs=[pl.BlockSpec((TB, D), lambda i: (i, 0)),
                  pl.BlockSpec((1, D), lambda i: (0, 0)),
                  pl.BlockSpec(wt.shape, lambda i: (0, 0)),
                  pl.BlockSpec(kgain.shape, lambda i: (0, 0))],
        out_specs=[bspec, rspec, bspec, bspec, pl.BlockSpec((TB, 8 * PAIR), lambda i: (i, 0)), bspec],
        out_shape=[blocked, rowmajor, blocked, blocked,
                   jax.ShapeDtypeStruct((T, 8 * PAIR), BF16), blocked],
        compiler_params=_params(1),
        name="even_inproj",
    )(x, gain, wt, kgain)


def _odd_inproj_kernel(x_ref, g_ref, wt_ref, kg_ref, q_ref, k_ref, v_ref, *, n_q, n_kv):
    h = _rms_rows(x_ref[...], g_ref[...]).astype(BF16)
    scale = HEAD_DIM ** -0.5 * LOG2E
    group = 8 * HEAD_DIM
    for g in range(n_q * HEAD_DIM // group):
        w = wt_ref[g * group:(g + 1) * group, :]
        t = lax.dot_general(w, h, _NT, preferred_element_type=F32)
        for hh in range(8):
            blk = t[hh * HEAD_DIM:(hh + 1) * HEAD_DIM, :]
            q_ref[g * 8 + hh, 0] = (blk * (_rms_scale_cols(blk) * scale)).astype(BF16)
    kv_rows = n_kv * HEAD_DIM
    w = wt_ref[n_q * HEAD_DIM:n_q * HEAD_DIM + 2 * kv_rows, :]
    t = lax.dot_general(w, h, _NT, preferred_element_type=F32)
    kparts = []
    for hh in range(n_kv):
        blk = t[hh * HEAD_DIM:(hh + 1) * HEAD_DIM, :]
        kparts.append(blk * _rms_scale_cols(blk))
        v_ref[hh, 0] = t[kv_rows + hh * HEAD_DIM:kv_rows + (hh + 1) * HEAD_DIM, :].astype(BF16)
    k_ref[...] = (jnp.concatenate(kparts, axis=0) * kg_ref[...]).T.astype(BF16)


def _odd_inproj(x, gain, wt, kgain, n_q, n_kv):
    T, D = x.shape
    nblk = T // TB
    return pl.pallas_call(
        functools.partial(_odd_inproj_kernel, n_q=n_q, n_kv=n_kv),
        grid=(nblk,),
        in_specs=[pl.BlockSpec((TB, D), lambda i: (i, 0)),
                  pl.BlockSpec((1, D), lambda i: (0, 0)),
                  pl.BlockSpec(wt.shape, lambda i: (0, 0)),
                  pl.BlockSpec(kgain.shape, lambda i: (0, 0))],
        out_specs=[pl.BlockSpec((n_q, 1, HEAD_DIM, TB), lambda i: (0, i, 0, 0)),
                   pl.BlockSpec((TB, n_kv * HEAD_DIM), lambda i: (i, 0)),
                   pl.BlockSpec((n_kv, 1, HEAD_DIM, TB), lambda i: (0, i, 0, 0))],
        out_shape=[jax.ShapeDtypeStruct((n_q, nblk, HEAD_DIM, TB), BF16),
                   jax.ShapeDtypeStruct((T, n_kv * HEAD_DIM), BF16),
                   jax.ShapeDtypeStruct((n_kv, nblk, HEAD_DIM, TB), BF16)],
        compiler_params=_params(1),
        name="odd_inproj",
    )(x, gain, wt, kgain)


def _sb_kernel(q_ref, k_ref, v_ref, o_ref, carry_ref, acc_ref, lb_ref, hl_ref, w_ref):
    i = pl.program_id(2)
    n_heads = q_ref.shape[0]
    key_idx = lax.broadcasted_iota(jnp.int32, (TB, TB), 0)
    qry_idx = lax.broadcasted_iota(jnp.int32, (TB, TB), 1)
    later = jnp.where(qry_idx > key_idx, 1.0, 0.0).astype(BF16)
    later2 = jnp.concatenate([later, later], axis=1)
    past = key_idx < qry_idx
    zeros = jnp.zeros((HEAD_DIM, TB), BF16)

    def group(j, diagonal):
        rows = pl.ds(pl.multiple_of(j * TB, TB), TB)
        colsums = []
        for hh in range(n_heads):
            q = q_ref[hh, 0]
            q_ext = jnp.concatenate([q, zeros] if hh % 2 == 0 else [zeros, q], axis=0)
            kj = k_ref[rows, (hh // 2) * PAIR:(hh // 2 + 1) * PAIR]
            z = jnp.dot(kj, q_ext, preferred_element_type=F32)
            log_beta = jnp.minimum(z, 0.0) - jnp.log2(1.0 + jnp.exp2(-jnp.abs(z)))
            log_not = log_beta - z
            if diagonal:
                log_not = jnp.where(past, log_not, 0.0)
            hi = log_not.astype(BF16)
            lb_ref[hh] = log_beta
            hl_ref[hh, :TB, :] = hi
            hl_ref[hh, TB:, :] = (log_not - hi.astype(F32)).astype(BF16)
            colsums.append(jnp.sum(log_not, axis=0, keepdims=True))
        carries = []
        for hh in range(n_heads):
            between = jnp.dot(later2, hl_ref[hh], preferred_element_type=F32)
            carry = 0.0 if diagonal else carry_ref[hh]
            w = jnp.exp2(lb_ref[hh] + between + carry)
            if diagonal:
                w = jnp.where(past, w, 0.0)
            w_ref[hh] = w.astype(BF16)
            carries.append(carry + colsums[hh])
            carry_ref[hh] = carries[hh]
        for hh in range(n_heads):
            pv = jnp.dot(v_ref[hh, j], w_ref[hh], preferred_element_type=F32)
            acc_ref[hh] = pv if diagonal else acc_ref[hh] + pv
        return carries

    def alive_flag(carries):
        top = functools.reduce(jnp.maximum, carries)
        return (jnp.max(top) > SB_DEAD_LOG2).astype(jnp.int32)

    alive = alive_flag(group(i, True))

    def cond(state):
        return jnp.logical_and(state[0] >= 0, state[1] > 0)

    def body(state):
        j = state[0]
        return j - 1, alive_flag(group(j, False))

    lax.while_loop(cond, body, (i - 1, alive))
    for hh in range(n_heads):
        o_ref[hh * HEAD_DIM:(hh + 1) * HEAD_DIM, :] = acc_ref[hh].astype(BF16)


def _sb_attention(qt, k, vt, batch):
    n_heads, nblk = qt.shape[0], qt.shape[1]
    nq = nblk // batch
    S = nq * TB
    step = SB_HEADS_PER_STEP
    assert n_heads % step == 0 and step % 2 == 0
    return pl.pallas_call(
        _sb_kernel,
        grid=(batch, n_heads // step, nq),
        in_specs=[pl.BlockSpec((step, 1, HEAD_DIM, TB), lambda b, h, i: (h, b * nq + i, 0, 0)),
                  pl.BlockSpec((S, step * HEAD_DIM), lambda b, h, i: (b, h)),
                  pl.BlockSpec((step, nq, HEAD_DIM, TB), lambda b, h, i: (h, b, 0, 0))],
        out_specs=pl.BlockSpec((step * HEAD_DIM, TB), lambda b, h, i: (h, b * nq + i)),
        out_shape=jax.ShapeDtypeStruct((n_heads * HEAD_DIM, nblk * TB), BF16),
        scratch_shapes=[pltpu.VMEM((step, 1, TB), F32), pltpu.VMEM((step, HEAD_DIM, TB), F32),
                        pltpu.VMEM((step, TB, TB), F32), pltpu.VMEM((step, 2 * TB, TB), BF16),
                        pltpu.VMEM((step, TB, TB), BF16)],
        compiler_params=_params(3),
        name="stickbreak_attn",
    )(qt, k, vt)


def _moba_kernel(q_ref, k_ref, v_ref, bias_ref, o_ref, kmean_ref, mask_ref, s_ref, p_ref, *,
                 n_tiles):
    n_heads, nq = q_ref.shape[0], q_ref.shape[1]
    heads = range(n_heads)

    def key_block(hh, j):
        return k_ref[pl.ds(pl.multiple_of(j * TB, TB), TB), hh * PAIR:(hh + 1) * PAIR]

    def block_means(jb, _):
        for hh in heads:
            blk = key_block(hh, jb).astype(F32)
            kmean_ref[hh, pl.ds(jb, 1), :] = jnp.mean(blk, axis=0, keepdims=True)
        return 0

    lax.fori_loop(0, nq, block_means, 0)

    blk_idx = lax.broadcasted_iota(jnp.int32, (nq, TB), 0)

    def choose(i, _):
        for hh in heads:
            q_gate = jnp.concatenate([q_ref[hh, i], jnp.zeros((PAIR - HEAD_DIM, TB), BF16)],
                                     axis=0)
            km = kmean_ref[hh]
            km_hi = km.astype(BF16)
            km_lo = (km - km_hi.astype(F32)).astype(BF16)
            gate = (jnp.dot(km_hi, q_gate, preferred_element_type=F32)
                    + jnp.dot(km_lo, q_gate, preferred_element_type=F32))
            gate = jnp.where(blk_idx < i, gate, NEG_INF)
            chosen = blk_idx == i
            for _ in range(MOBA_TOPK):
                best = jnp.max(gate, axis=0, keepdims=True)
                cand = jnp.where(jnp.logical_and(gate == best, best > 0.5 * NEG_INF),
                                 blk_idx, nq)
                pick = blk_idx == jnp.min(cand, axis=0, keepdims=True)
                chosen = jnp.logical_or(chosen, pick)
                gate = jnp.where(pick, NEG_INF, gate)
            mask_ref[hh, i] = jnp.where(chosen, 0.0, NEG_INF)
        return 0

    lax.fori_loop(0, nq, choose, 0)

    first_row = lax.broadcasted_iota(jnp.int32, (16, TB), 0) == 0
    ones_row = jnp.where(first_row, 1.0, 0.0).astype(BF16)
    q_tail = jnp.zeros((PAIR - HEAD_DIM - 16, TB), BF16)

    def following(item):
        i, g = item
        wrap = g == i // MOBA_UNROLL
        stay = jnp.logical_and(wrap, i == nq - 1)
        return (jnp.where(jnp.logical_and(wrap, jnp.logical_not(stay)), i + 1, i),
                jnp.where(stay, g, jnp.where(wrap, 0, g + 1)))

    def score_group(hh, item, slot):
        i, g = item
        q = q_ref[hh, i]
        raw = []
        for u in range(MOBA_UNROLL):
            j = g * MOBA_UNROLL + u
            mask_rows = jnp.where(first_row, mask_ref[hh, i, pl.ds(j, 1), :], 0.0).astype(BF16)
            q_ext = jnp.concatenate([q, mask_rows, q_tail], axis=0)
            raw.append(jnp.dot(key_block(hh, j), q_ext, preferred_element_type=F32))
        top = jnp.full((1, TB), NEG_INF, F32)
        for u in range(MOBA_UNROLL):
            tile = jnp.clip(i - (g * MOBA_UNROLL + u), 0, n_tiles - 1)
            s = raw[u] + bias_ref[hh, tile]
            s_ref[hh, slot, u] = s
            top = jnp.maximum(top, jnp.max(s, axis=0, keepdims=True))
        return top

    def value_group(hh, item, slot, top, state):
        i, g = item
        fresh = g == 0
        m = jnp.where(fresh, NEG_INF, state[0])
        acc = jnp.where(fresh, 0.0, state[1])
        m_new = jnp.maximum(m, top)
        for u in range(MOBA_UNROLL):
            p_ref[hh, slot, u] = jnp.exp2(s_ref[hh, slot, u] - m_new).astype(BF16)
        pv = None
        for u in range(MOBA_UNROLL):
            v_ext = jnp.concatenate([v_ref[hh, g * MOBA_UNROLL + u], ones_row], axis=0)
            part = jnp.dot(v_ext, p_ref[hh, slot, u], preferred_element_type=F32)
            pv = part if pv is None else pv + part
        acc = jnp.exp2(m - m_new) * acc + pv
        o_ref[hh, i] = (acc[:HEAD_DIM] / acc[HEAD_DIM:HEAD_DIM + 1]).astype(BF16)
        return m_new, acc

    def pair(_, carry):
        item_a, tops_a, states = carry
        item_b = following(item_a)
        item_c = following(item_b)
        tops_b = [score_group(hh, item_b, 1) for hh in heads]
        states = [value_group(hh, item_a, 0, tops_a[hh], states[hh]) for hh in heads]
        tops_c = [score_group(hh, item_c, 0) for hh in heads]
        states = [value_group(hh, item_b, 1, tops_b[hh], states[hh]) for hh in heads]
        return item_c, tops_c, states

    n_groups = nq // MOBA_UNROLL
    n_items = MOBA_UNROLL * n_groups * (n_groups + 1) // 2
    first = (jnp.int32(0), jnp.int32(0))
    init = [(jnp.full((1, TB), NEG_INF, F32), jnp.zeros((HEAD_DIM + 16, TB), F32))
            for _ in heads]
    lax.fori_loop(0, n_items // 2, pair, (first, [score_group(hh, first, 0) for hh in heads], init))


def _moba_attention(qt, k, vt, bias_tiles, batch):
    n_heads, nblk = qt.shape[0], qt.shape[1]
    nq = nblk // batch
    S = nq * TB
    n_tiles = bias_tiles.shape[1]
    step = MOBA_HEADS_PER_STEP
    assert n_heads % step == 0 and nq % MOBA_UNROLL == 0
    return pl.pallas_call(
        functools.partial(_moba_kernel, n_tiles=n_tiles),
        grid=(n_heads // step, batch),
        in_specs=[pl.BlockSpec((step, nq, HEAD_DIM, TB), lambda h, b: (h, b, 0, 0)),
                  pl.BlockSpec((S, step * PAIR), lambda h, b: (b, h)),
                  pl.BlockSpec((step, nq, HEAD_DIM, TB), lambda h, b: (h, b, 0, 0)),
                  pl.BlockSpec((step, n_tiles, TB, TB), lambda h, b: (h, 0, 0, 0),
                               pipeline_mode=pl.Buffered(1))],
        out_specs=pl.BlockSpec((step, nq, HEAD_DIM, TB), lambda h, b: (h, b, 0, 0)),
        out_shape=jax.ShapeDtypeStruct((n_heads, nblk, HEAD_DIM, TB), BF16),
        scratch_shapes=[pltpu.VMEM((step, nq, PAIR), F32), pltpu.VMEM((step, nq, nq, TB), F32),
                        pltpu.VMEM((step, 2, MOBA_UNROLL, TB, TB), F32),
                        pltpu.VMEM((step, 2, MOBA_UNROLL, TB, TB), BF16)],
        compiler_params=_params(2),
        name="moba_attn",
    )(qt, k, vt, bias_tiles)


def _swa_kernel(sink_ref, q_ref, kp_ref, kc_ref, vp_ref, vc_ref, bias_ref, o_ref, s_ref, p_ref):
    kv_head = pl.program_id(0)
    i = pl.program_id(2)
    group = q_ref.shape[0]
    w = SWA_WINDOW
    kp, kc = kp_ref[...], kc_ref[...]
    vp, vc = vp_ref[0, 0], vc_ref[0, 0]
    keys = [jnp.concatenate([kp[TB - w:, :], kc[:w, :]], axis=0), kc]
    vals = [jnp.concatenate([vp[:, TB - w:], vc[:, :w]], axis=1), vc]
    first_rows_exist = jnp.logical_or(
        lax.broadcasted_iota(jnp.int32, (2 * w, w), 0) >= w, i > 0)
    n_halves = TB // w
    sinks = [sink_ref[kv_head * group + hh] * LOG2E for hh in range(group)]
    maxes, denoms = [], []
    for hh in range(group):
        q_ext = _pair_extend(q_ref[hh, 0], kv_head)
        for u in range(n_halves):
            s = (jnp.dot(keys[u], q_ext[:, u * w:(u + 1) * w], preferred_element_type=F32)
                 + bias_ref[hh])
            if u == 0:
                s = jnp.where(first_rows_exist, s, NEG_INF)
            s_ref[hh * n_halves + u] = s
            maxes.append(jnp.maximum(jnp.max(s, axis=0, keepdims=True), sinks[hh]))
    for c in range(group * n_halves):
        p = jnp.exp2(s_ref[c] - maxes[c])
        denoms.append(jnp.sum(p, axis=0, keepdims=True) + jnp.exp2(sinks[c // n_halves] - maxes[c]))
        p_ref[c] = p.astype(BF16)
    for hh in range(group):
        halves = [jnp.dot(vals[u], p_ref[hh * n_halves + u], preferred_element_type=F32)
                  / denoms[hh * n_halves + u] for u in range(n_halves)]
        o_ref[hh * HEAD_DIM:(hh + 1) * HEAD_DIM, :] = jnp.concatenate(halves, axis=1).astype(BF16)


def _swa_attention(qt, k, vt, bias_tiles, sinks, batch):
    n_heads, nblk = qt.shape[0], qt.shape[1]
    n_kv = vt.shape[0]
    group = n_heads // n_kv
    nq = nblk // batch
    assert TB == 2 * SWA_WINDOW

    def prev(b, i):
        return b * nq + jnp.maximum(i - 1, 0)

    return pl.pallas_call(
        _swa_kernel,
        grid_spec=pltpu.PrefetchScalarGridSpec(
            num_scalar_prefetch=1,
            grid=(n_kv, batch, nq),
            in_specs=[
                pl.BlockSpec((group, 1, HEAD_DIM, TB), lambda g, b, i, s: (g, b * nq + i, 0, 0)),
                pl.BlockSpec((TB, PAIR), lambda g, b, i, s: (prev(b, i), g // 2)),
                pl.BlockSpec((TB, PAIR), lambda g, b, i, s: (b * nq + i, g // 2)),
                pl.BlockSpec((1, 1, HEAD_DIM, TB), lambda g, b, i, s: (g, prev(b, i), 0, 0)),
                pl.BlockSpec((1, 1, HEAD_DIM, TB), lambda g, b, i, s: (g, b * nq + i, 0, 0)),
                pl.BlockSpec((group, 2 * SWA_WINDOW, SWA_WINDOW), lambda g, b, i, s: (g, 0, 0)),
            ],
            out_specs=pl.BlockSpec((group * HEAD_DIM, TB), lambda g, b, i, s: (g, b * nq + i)),
            scratch_shapes=[pltpu.VMEM((2 * group, 2 * SWA_WINDOW, SWA_WINDOW), F32),
                            pltpu.VMEM((2 * group, 2 * SWA_WINDOW, SWA_WINDOW), BF16)],
        ),
        out_shape=jax.ShapeDtypeStruct((n_heads * HEAD_DIM, nblk * TB), BF16),
        compiler_params=_params(3),
        name="swa_attn",
    )(sinks, qt, k, k, vt, vt, bias_tiles)


def _mem_kv_kernel(mem_ref, g_ref, wt_ref, kg_ref, k_ref, v_ref):
    h = _rms_rows(mem_ref[0], g_ref[0]).astype(BF16)
    t = lax.dot_general(wt_ref[0], h, _NT, preferred_element_type=F32)
    cw = CROSS_HEADS * CROSS_HEAD_DIM
    for hh in range(CROSS_HEADS):
        blk = t[hh * CROSS_HEAD_DIM:(hh + 1) * CROSS_HEAD_DIM, :]
        kn = blk * _rms_scale_cols(blk) * kg_ref[0]
        k_ref[0, 0, hh] = kn.T.astype(BF16)
        v_ref[0, 0, hh] = t[cw + hh * CROSS_HEAD_DIM:cw + (hh + 1) * CROSS_HEAD_DIM, :].astype(BF16)


def _mem_kv(mem, gains, wts, kgains):
    B, M, D = mem.shape
    depth = gains.shape[0]
    kshape = (depth, B, CROSS_HEADS, M, CROSS_HEAD_DIM)
    vshape = (depth, B, CROSS_HEADS, CROSS_HEAD_DIM, M)
    return pl.pallas_call(
        _mem_kv_kernel,
        grid=(depth, B),
        in_specs=[pl.BlockSpec((1, M, D), lambda l, b: (b, 0, 0)),
                  pl.BlockSpec((1, 1, D), lambda l, b: (l, 0, 0)),
                  pl.BlockSpec((1,) + wts.shape[1:], lambda l, b: (l, 0, 0)),
                  pl.BlockSpec((1,) + kgains.shape[1:], lambda l, b: (l, 0, 0))],
        out_specs=[pl.BlockSpec((1, 1) + kshape[2:], lambda l, b: (l, b, 0, 0, 0)),
                   pl.BlockSpec((1, 1) + vshape[2:], lambda l, b: (l, b, 0, 0, 0))],
        out_shape=[jax.ShapeDtypeStruct(kshape, BF16), jax.ShapeDtypeStruct(vshape, BF16)],
        compiler_params=_params(2),
        name="mem_kv",
    )(mem, gains, wts, kgains)


def _post_kernel(*refs, n_attn):
    x_ref = refs[0]
    attn_refs = refs[1:1 + n_attn]
    wout_refs = refs[1 + n_attn:1 + 2 * n_attn]
    g_ref, wq_ref, k_ref, v_ref, wo_ref, o_ref, h_ref, qn_ref, p_ref, ot_ref = refs[1 + 2 * n_attn:]
    subs = [slice(s * TB, (s + 1) * TB) for s in range(x_ref.shape[0] // TB)]
    hd = CROSS_HEAD_DIM
    for s, rows in enumerate(subs):
        x = x_ref[rows, :]
        for a_ref, w_ref in zip(attn_refs, wout_refs):
            if len(a_ref.shape) == 2:
                a = a_ref[:, rows]
            else:
                a = jnp.concatenate([a_ref[hh, s] for hh in range(a_ref.shape[0])], axis=0)
            a = a.astype(F32).T.astype(BF16)
            x = x + jnp.dot(a, w_ref[...], preferred_element_type=F32)
        o_ref[rows, :] = x
        h_ref[rows, :] = _rms_rows(x, g_ref[...]).astype(BF16)
    scale = hd ** -0.5 * LOG2E
    for cols in subs:
        qt = lax.dot_general(wq_ref[...], h_ref[cols, :], _NT, preferred_element_type=F32)
        for hh in range(CROSS_HEADS):
            blk = qt[hh * hd:(hh + 1) * hd, :]
            qn_ref[hh, :, cols] = (blk * (_rms_scale_cols(blk) * scale)).astype(BF16)
    denoms = {}
    for s, cols in enumerate(subs):
        for hh in range(CROSS_HEADS):
            sc = jnp.dot(k_ref[0, 0, hh], qn_ref[hh, :, cols], preferred_element_type=F32)
            p = jnp.exp2(sc - jnp.max(sc, axis=0, keepdims=True))
            denoms[s, hh] = jnp.sum(p, axis=0, keepdims=True)
            p_ref[hh, :, cols] = p.astype(BF16)
    for s, cols in enumerate(subs):
        for hh in range(CROSS_HEADS):
            o = jnp.dot(v_ref[0, 0, hh], p_ref[hh, :, cols], preferred_element_type=F32)
            ot_ref[hh * hd:(hh + 1) * hd, cols] = o / denoms[s, hh]
    for rows in subs:
        o_ref[rows, :] += jnp.dot(ot_ref[:, rows].T.astype(BF16), wo_ref[...],
                                  preferred_element_type=F32)


def _post_mixer(x, attn_list, wout_list, gain, wq_t, mem_k, mem_v, wo, layer, seq):
    T, D = x.shape
    tile = POST_TILE
    per_seq = seq // tile
    n_attn = len(attn_list)
    const = lambda i: (0, 0)
    in_specs = [pl.BlockSpec((tile, D), lambda i: (i, 0))]
    for a in attn_list:
        if a.ndim == 2:
            in_specs.append(pl.BlockSpec((a.shape[0], tile), lambda i: (0, i)))
        else:
            in_specs.append(pl.BlockSpec((a.shape[0], tile // TB) + a.shape[2:],
                                         lambda i: (0, i, 0, 0)))
    in_specs += [pl.BlockSpec(w.shape, const) for w in wout_list]
    in_specs += [pl.BlockSpec((1, D), const), pl.BlockSpec(wq_t.shape, const),
                 pl.BlockSpec((1, 1) + mem_k.shape[2:], lambda i: (layer, i // per_seq, 0, 0, 0)),
                 pl.BlockSpec((1, 1) + mem_v.shape[2:], lambda i: (layer, i // per_seq, 0, 0, 0)),
                 pl.BlockSpec(wo.shape, const)]
    return pl.pallas_call(
        functools.partial(_post_kernel, n_attn=n_attn),
        grid=(T // tile,),
        in_specs=in_specs,
        out_specs=pl.BlockSpec((tile, D), lambda i: (i, 0)),
        out_shape=jax.ShapeDtypeStruct((T, D), F32),
        scratch_shapes=[pltpu.VMEM((tile, D), BF16),
                        pltpu.VMEM((CROSS_HEADS, CROSS_HEAD_DIM, tile), BF16),
                        pltpu.VMEM((CROSS_HEADS, mem_k.shape[3], tile), BF16),
                        pltpu.VMEM((CROSS_HEADS * CROSS_HEAD_DIM, tile), F32)],
        compiler_params=_params(1),
        name="outproj_cross",
    )(x, *attn_list, *wout_list, gain, wq_t, mem_k, mem_v, wo)


def _ffn_kernel(x_ref, g_ref, w1_ref, cw_ref, w2_ref, o_ref, tail_ref, ext_ref, h_ref, act_ref, *,
                per_seq):
    i = pl.program_id(0)
    tile = x_ref.shape[0]
    n_chunks = w2_ref.shape[0]
    lead = 8
    x = x_ref[...]
    h_ref[...] = _rms_rows(x, g_ref[...]).astype(BF16)

    @pl.when(i % per_seq == 0)
    def _():
        tail_ref[...] = jnp.zeros(tail_ref.shape, F32)

    o_ref[...] = x

    def project(c, slot):
        for part in range(2):
            cc = c + part * n_chunks
            u = jnp.dot(h_ref[...], w1_ref[cc], preferred_element_type=F32)
            ext_ref[slot, part, 0:lead, :] = tail_ref[cc]
            ext_ref[slot, part, lead:, :] = u
            tail_ref[cc] = u[tile - lead:, :]

    def gate_up(c, slot):
        halves = []
        for part in range(2):
            cw = cw_ref[c + part * n_chunks]
            ext = ext_ref.at[slot, part]
            halves.append(cw[3:4, :] + cw[0:1, :] * ext[lead - 2:lead - 2 + tile, :]
                          + cw[1:2, :] * ext[lead - 1:lead - 1 + tile, :]
                          + cw[2:3, :] * ext[lead:, :])
        gate, up = halves
        act_ref[c] = (gate * (1.0 / (1.0 + jnp.exp(-gate))) * up).astype(BF16)

    def down(group):
        a = jnp.concatenate([act_ref[c] for c in group], axis=1)
        w = jnp.concatenate([w2_ref[c] for c in group], axis=0)
        o_ref[...] += jnp.dot(a, w, preferred_element_type=F32)

    groups = [list(range(s, min(s + FFN_GROUP, n_chunks))) for s in range(0, n_chunks, FFN_GROUP)]

    def slot(gi, k):
        return (gi % 2) * FFN_GROUP + k

    for k, c in enumerate(groups[0]):
        project(c, slot(0, k))
    for gi, group in enumerate(groups):
        if gi + 1 < len(groups):
            for k, c in enumerate(groups[gi + 1]):
                project(c, slot(gi + 1, k))
        for k, c in enumerate(group):
            gate_up(c, slot(gi, k))
        down(group)


def _conv_ffn(x, gain, w1, cw, w2, seq):
    T, D = x.shape
    tile = FFN_TILE
    const3 = lambda i: (0, 0, 0)
    return pl.pallas_call(
        functools.partial(_ffn_kernel, per_seq=seq // tile),
        grid=(T // tile,),
        in_specs=[pl.BlockSpec((tile, D), lambda i: (i, 0)),
                  pl.BlockSpec((1, D), lambda i: (0, 0)),
                  pl.BlockSpec(w1.shape, const3, pipeline_mode=pl.Buffered(1)),
                  pl.BlockSpec(cw.shape, const3),
                  pl.BlockSpec(w2.shape, const3, pipeline_mode=pl.Buffered(1))],
        out_specs=pl.BlockSpec((tile, D), lambda i: (i, 0)),
        out_shape=jax.ShapeDtypeStruct((T, D), F32),
        scratch_shapes=[pltpu.VMEM((w1.shape[0], 8, FFN_CHUNK), F32),
                        pltpu.VMEM((2 * FFN_GROUP, 2, tile + 8, FFN_CHUNK), F32),
                        pltpu.VMEM((tile, D), BF16),
                        pltpu.VMEM((w2.shape[0], tile, FFN_CHUNK), BF16)],
        compiler_params=_params(1),
        name="conv_ffn",
    )(x, gain, w1, cw, w2)


def _t5_bucket(dist):
    n = jnp.maximum(dist, 0)
    max_exact = NUM_BUCKETS // 2
    nf = jnp.maximum(n, 1).astype(F32)
    coef = (NUM_BUCKETS - max_exact) / math.log(MAX_DISTANCE / max_exact)
    large = max_exact + (jnp.log(nf / max_exact) * coef).astype(jnp.int32)
    large = jnp.minimum(large, NUM_BUCKETS - 1)
    return jnp.where(n < max_exact, n, large)


def _toeplitz_kernel(w_ref, o_ref, *, window):
    rows, cols = o_ref.shape[-2:]
    period = w_ref.shape[-1]
    spread = jnp.broadcast_to(w_ref[0, 0], (rows, period))
    skew = pltpu.roll(spread, 0, 1, stride=1, stride_axis=0)
    key = lax.broadcasted_iota(jnp.int32, (rows, cols), 0)
    qry = lax.broadcasted_iota(jnp.int32, (rows, cols), 1)
    if window is None:
        keep = jnp.logical_or(qry >= key, pl.program_id(1) > 0)
    else:
        keep = jnp.logical_and(qry - key + window >= 0, qry - key < 0)
    o_ref[0, 0] = jnp.where(keep, skew[:, rows:rows + cols], NEG_INF)


def _toeplitz_tiles(windows, rows, cols, window):
    n_heads, n_tiles, _, period = windows.shape
    assert period >= rows + cols and rows % 128 == 0 and cols % 128 == 0 and period % 128 == 0
    return pl.pallas_call(
        functools.partial(_toeplitz_kernel, window=window),
        grid=(n_heads, n_tiles),
        in_specs=[pl.BlockSpec((1, 1, 1, period), lambda h, d: (h, d, 0, 0))],
        out_specs=pl.BlockSpec((1, 1, rows, cols), lambda h, d: (h, d, 0, 0)),
        out_shape=jax.ShapeDtypeStruct((n_heads, n_tiles, rows, cols), F32),
        compiler_params=_params(2),
        name="bias_tiles",
    )(windows)


def _moba_bias_tiles(bias_by_dist):
    n_tiles = min(bias_by_dist.shape[1] // TB, -(-(MAX_DISTANCE + TB) // TB))
    period = 3 * TB
    padded = jnp.pad(bias_by_dist, ((0, 0), (TB, period)))
    windows = jnp.stack([padded[:, d * TB:d * TB + period] for d in range(n_tiles)], axis=1)
    return _toeplitz_tiles(windows[:, :, None, :], TB, TB, None)


def _swa_bias_tiles(bias_by_dist):
    w = SWA_WINDOW
    windows = jnp.pad(bias_by_dist[:, :3 * w], ((0, 0), (w, 0)))
    return _toeplitz_tiles(windows[:, None, None, :], 2 * w, w, w)[:, 0]


def kernel(x, mem, rel_bias, mix_norm, ev_w_in, ev_w_out, ev_q_gain, ev_k_gain, od_w_in, od_w_out, od_q_gain, od_k_gain, od_sinks, cx_norm, cx_mem_norm, cx_w_q, cx_w_kv, cx_w_o, cx_q_gain, cx_k_gain, ff_norm, ff_w_in, ff_conv_w, ff_conv_b, ff_w_out):
    B, S, D = x.shape
    T = B * S
    depth = mix_norm.shape[0]
    n_heads = rel_bias.shape[1]
    sb_heads = n_heads // 2
    d_ff = ff_w_out.shape[1]
    n_chunks = d_ff // FFN_CHUNK
    n_kv = (od_w_in.shape[2] // HEAD_DIM - n_heads) // 2
    assert S % FFN_TILE == 0 and S % POST_TILE == 0 and S % TB == 0 and d_ff % FFN_CHUNK == 0

    def lane_bcast(col, width):
        return jnp.broadcast_to(col[:, None], (col.shape[0], width)).astype(F32)

    bias_by_dist = rel_bias.astype(F32)[_t5_bucket(jnp.arange(S + TB, dtype=jnp.int32))].T
    moba_tiles = _moba_bias_tiles(bias_by_dist[sb_heads:, :S] * LOG2E)
    swa_tiles = _swa_bias_tiles(bias_by_dist * LOG2E)

    cx_kgain = jnp.stack([lane_bcast(cx_k_gain[l] * cx_q_gain[l], mem.shape[1]) for l in range(depth)])
    mem_k, mem_v = _mem_kv(mem, cx_mem_norm[:, None, :], jnp.swapaxes(cx_w_kv, 1, 2).astype(BF16),
                           cx_kgain)

    xf = x.reshape(T, D)
    for layer in range(depth):
        idx = layer // 2
        gain = mix_norm[layer][None, :]
        if layer % 2 == 0:
            kgain = lane_bcast(jnp.tile(ev_k_gain[idx] * ev_q_gain[idx], sb_heads), TB)
            qa, ka, va, qb, kb, vb = _even_inproj(xf, gain, ev_w_in[idx].T.astype(BF16), kgain)
            attn = [_sb_attention(qa, ka, va, B), _moba_attention(qb, kb, vb, moba_tiles, B)]
            half = sb_heads * HEAD_DIM
            wout = [ev_w_out[idx, :half].astype(BF16), ev_w_out[idx, half:].astype(BF16)]
        else:
            kgain = lane_bcast(jnp.tile(od_k_gain[idx] * od_q_gain[idx], n_kv), TB)
            q, k, v = _odd_inproj(xf, gain, od_w_in[idx].T.astype(BF16), kgain, n_heads, n_kv)
            attn = [_swa_attention(q, k, v, swa_tiles, od_sinks[idx].astype(F32), B)]
            wout = [od_w_out[idx].astype(BF16)]
        xf = _post_mixer(xf, attn, wout, cx_norm[layer][None, :], cx_w_q[layer].T.astype(BF16),
                         mem_k, mem_v, cx_w_o[layer].astype(BF16), layer, S)
        w1 = ff_w_in[layer].reshape(D, 2 * n_chunks, FFN_CHUNK).transpose(1, 0, 2).astype(BF16)
        cw = jnp.concatenate([ff_conv_w[layer], ff_conv_b[layer][None, :]], axis=0)
        cw = cw.reshape(CONV_WIDTH + 1, 2 * n_chunks, FFN_CHUNK).transpose(1, 0, 2)
        w2 = ff_w_out[layer].reshape(n_chunks, FFN_CHUNK, D).astype(BF16)
        xf = _conv_ffn(xf, ff_norm[layer][None, :], w1, cw, w2, S)
    return xf.reshape(B, S, D)
```

```python
import functools
import math

import jax
import jax.numpy as jnp
import numpy as np
from jax import lax
from jax.experimental import pallas as pl
from jax.experimental.pallas import tpu as pltpu

F32 = jnp.float32
BF16 = jnp.bfloat16

HEAD_DIM = 64
PAIR = 2 * HEAD_DIM
TB = 256
SWA_WINDOW = 128
MOBA_TOPK = 3
MOBA_UNROLL = 4
MOBA_HEADS_PER_STEP = 2
MOBA_PAIRS_PER_TRIP = 2
LOG2E = math.log2(math.e)
NUM_BUCKETS = 32
MAX_DISTANCE = 4096
CROSS_HEADS = 4
CROSS_HEAD_DIM = 128
CONV_WIDTH = 3
RMS_EPS = 1e-6
NEG_INF = -1e30
SB_DEAD_LOG2 = -150.5
SB_HEADS_PER_STEP = 8
FFN_CHUNK = 256
FFN_TILE = 512
FFN_GROUP = 6
POST_TILE = 512
VMEM_LIMIT = 56 * 1024 * 1024

_NT = (((1,), (1,)), ((), ()))
_TN = (((0,), (0,)), ((), ()))


def _rms_rows(x, gain):
    ms = jnp.mean(x * x, axis=-1, keepdims=True)
    return (x * lax.rsqrt(ms + RMS_EPS)) * gain


def _rms_scale_cols(xt):
    return lax.rsqrt(jnp.mean(xt * xt, axis=0, keepdims=True) + RMS_EPS)


def _params(n_axes):
    return pltpu.CompilerParams(dimension_semantics=("arbitrary",) * n_axes,
                                vmem_limit_bytes=VMEM_LIMIT)


def _pair_extend(q, head):
    odd = (head % 2).astype(F32)
    q32 = q.astype(F32)
    return jnp.concatenate([q32 * (1.0 - odd), q32 * odd], axis=0).astype(BF16)


def _even_inproj_kernel(x_ref, g_ref, wt_ref, kg_ref,
                        qa_ref, ka_ref, va_ref, qb_ref, kb_ref, vb_ref):
    h = _rms_rows(x_ref[...], g_ref[...]).astype(BF16)
    width = 8 * HEAD_DIM

    def proj(group):
        w = wt_ref[group * width:(group + 1) * width, :]
        return lax.dot_general(w, h, _NT, preferred_element_type=F32)

    def heads(t):
        return [t[hh * HEAD_DIM:(hh + 1) * HEAD_DIM, :] for hh in range(8)]

    scale = HEAD_DIM ** -0.5
    for hh, blk in enumerate(heads(proj(0))):
        qa_ref[hh, 0] = (blk * (scale * LOG2E)).astype(BF16)
    ka_ref[...] = proj(1).T.astype(BF16)
    for hh, blk in enumerate(heads(proj(2))):
        va_ref[hh, 0] = blk.astype(BF16)
    for hh, blk in enumerate(heads(proj(3))):
        qb_ref[hh, 0] = (blk * (_rms_scale_cols(blk) * (scale * LOG2E))).astype(BF16)
    one_then_zeros = jnp.where(lax.broadcasted_iota(jnp.int32, (HEAD_DIM, TB), 0) == 0, 1.0, 0.0)
    kg = kg_ref[...]
    kb = []
    for hh, blk in enumerate(heads(proj(4))):
        kb += [blk * _rms_scale_cols(blk) * kg[hh * HEAD_DIM:(hh + 1) * HEAD_DIM, :], one_then_zeros]
    kb_ref[...] = jnp.concatenate(kb, axis=0).T.astype(BF16)
    for hh, blk in enumerate(heads(proj(5))):
        vb_ref[hh, 0] = blk.astype(BF16)


def _even_inproj(x, gain, wt, kgain):
    T, D = x.shape
    nblk = T // TB
    blocked = jax.ShapeDtypeStruct((8, nblk, HEAD_DIM, TB), BF16)
    rowmajor = jax.ShapeDtypeStruct((T, 8 * HEAD_DIM), BF16)
    bspec = pl.BlockSpec((8, 1, HEAD_DIM, TB), lambda i: (0, i, 0, 0))
    rspec = pl.BlockSpec((TB, 8 * HEAD_DIM), lambda i: (i, 0))
    return pl.pallas_call(
        _even_inproj_kernel,
        grid=(nblk,),
        in_specs=[pl.BlockSpec((TB, D), lambda i: (i, 0)),
                  pl.BlockSpec((1, D), lambda i: (0, 0)),
                  pl.BlockSpec(wt.shape, lambda i: (0, 0)),
                  pl.BlockSpec(kgain.shape, lambda i: (0, 0))],
        out_specs=[bspec, rspec, bspec, bspec, pl.BlockSpec((TB, 8 * PAIR), lambda i: (i, 0)), bspec],
        out_shape=[blocked, rowmajor, blocked, blocked,
                   jax.ShapeDtypeStruct((T, 8 * PAIR), BF16), blocked],
        compiler_params=_params(1),
        name="even_inproj",
    )(x, gain, wt, kgain)


def _odd_inproj_kernel(x_ref, g_ref, wt_ref, kg_ref, q_ref, k_ref, v_ref, *, n_q, n_kv):
    h = _rms_rows(x_ref[...], g_ref[...]).astype(BF16)
    scale = HEAD_DIM ** -0.5 * LOG2E
    group = 8 * HEAD_DIM
    for g in range(n_q * HEAD_DIM // group):
        w = wt_ref[g * group:(g + 1) * group, :]
        t = lax.dot_general(w, h, _NT, preferred_element_type=F32)
        for hh in range(8):
            blk = t[hh * HEAD_DIM:(hh + 1) * HEAD_DIM, :]
            q_ref[g * 8 + hh, 0] = (blk * (_rms_scale_cols(blk) * scale)).astype(BF16)
    kv_rows = n_kv * HEAD_DIM
    w = wt_ref[n_q * HEAD_DIM:n_q * HEAD_DIM + 2 * kv_rows, :]
    t = lax.dot_general(w, h, _NT, preferred_element_type=F32)
    kparts = []
    for hh in range(n_kv):
        blk = t[hh * HEAD_DIM:(hh + 1) * HEAD_DIM, :]
        kparts.append(blk * _rms_scale_cols(blk))
        v_ref[hh, 0] = t[kv_rows + hh * HEAD_DIM:kv_rows + (hh + 1) * HEAD_DIM, :].astype(BF16)
    k_ref[...] = (jnp.concatenate(kparts, axis=0) * kg_ref[...]).T.astype(BF16)


def _odd_inproj(x, gain, wt, kgain, n_q, n_kv):
    T, D = x.shape
    nblk = T // TB
    return pl.pallas_call(
        functools.partial(_odd_inproj_kernel, n_q=n_q, n_kv=n_kv),
        grid=(nblk,),
        in_specs=[pl.BlockSpec((TB, D), lambda i: (i, 0)),
                  pl.BlockSpec((1, D), lambda i: (0, 0)),
                  pl.BlockSpec(wt.shape, lambda i: (0, 0)),
                  pl.BlockSpec(kgain.shape, lambda i: (0, 0))],
        out_specs=[pl.BlockSpec((n_q, 1, HEAD_DIM, TB), lambda i: (0, i, 0, 0)),
                   pl.BlockSpec((TB, n_kv * HEAD_DIM), lambda i: (i, 0)),
                   pl.BlockSpec((n_kv, 1, HEAD_DIM, TB), lambda i: (0, i, 0, 0))],
        out_shape=[jax.ShapeDtypeStruct((n_q, nblk, HEAD_DIM, TB), BF16),
                   jax.ShapeDtypeStruct((T, n_kv * HEAD_DIM), BF16),
                   jax.ShapeDtypeStruct((n_kv, nblk, HEAD_DIM, TB), BF16)],
        compiler_params=_params(1),
        name="odd_inproj",
    )(x, gain, wt, kgain)


def _sb_kernel(q_ref, k_ref, v_ref, o_ref, carry_ref, acc_ref, lb_ref, hl_ref, w_ref):
    i = pl.program_id(2)
    n_heads = q_ref.shape[0]
    key_idx = lax.broadcasted_iota(jnp.int32, (TB, TB), 0)
    qry_idx = lax.broadcasted_iota(jnp.int32, (TB, TB), 1)
    later = jnp.where(qry_idx > key_idx, 1.0, 0.0).astype(BF16)
    later2 = jnp.concatenate([later, later], axis=1)
    past = key_idx < qry_idx
    zeros = jnp.zeros((HEAD_DIM, TB), BF16)

    def group(j, diagonal):
        rows = pl.ds(pl.multiple_of(j * TB, TB), TB)
        colsums = []
        for hh in range(n_heads):
            q = q_ref[hh, 0]
            q_ext = jnp.concatenate([q, zeros] if hh % 2 == 0 else [zeros, q], axis=0)
            kj = k_ref[rows, (hh // 2) * PAIR:(hh // 2 + 1) * PAIR]
            z = jnp.dot(kj, q_ext, preferred_element_type=F32)
            log_beta = jnp.minimum(z, 0.0) - jnp.log2(1.0 + jnp.exp2(-jnp.abs(z)))
            log_not = log_beta - z
            if diagonal:
                log_not = jnp.where(past, log_not, 0.0)
            hi = log_not.astype(BF16)
            lb_ref[hh] = log_beta
            hl_ref[hh, :TB, :] = hi
            hl_ref[hh, TB:, :] = (log_not - hi.astype(F32)).astype(BF16)
            colsums.append(jnp.sum(log_not, axis=0, keepdims=True))
        carries = []
        for hh in range(n_heads):
            between = jnp.dot(later2, hl_ref[hh], preferred_element_type=F32)
            carry = 0.0 if diagonal else carry_ref[hh]
            w = jnp.exp2(lb_ref[hh] + between + carry)
            if diagonal:
                w = jnp.where(past, w, 0.0)
            w_ref[hh] = w.astype(BF16)
            carries.append(carry + colsums[hh])
            carry_ref[hh] = carries[hh]
        for hh in range(n_heads):
            pv = jnp.dot(v_ref[hh, j], w_ref[hh], preferred_element_type=F32)
            acc_ref[hh] = pv if diagonal else acc_ref[hh] + pv
        return carries

    def alive_flag(carries):
        top = functools.reduce(jnp.maximum, carries)
        return (jnp.max(top) > SB_DEAD_LOG2).astype(jnp.int32)

    alive = alive_flag(group(i, True))

    def cond(state):
        return jnp.logical_and(state[0] >= 0, state[1] > 0)

    def body(state):
        j = state[0]
        return j - 1, alive_flag(group(j, False))

    lax.while_loop(cond, body, (i - 1, alive))
    for hh in range(n_heads):
        o_ref[hh * HEAD_DIM:(hh + 1) * HEAD_DIM, :] = acc_ref[hh].astype(BF16)


def _sb_attention(qt, k, vt, batch):
    n_heads, nblk = qt.shape[0], qt.shape[1]
    nq = nblk // batch
    S = nq * TB
    step = SB_HEADS_PER_STEP
    assert n_heads % step == 0 and step % 2 == 0
    return pl.pallas_call(
        _sb_kernel,
        grid=(batch, n_heads // step, nq),
        in_specs=[pl.BlockSpec((step, 1, HEAD_DIM, TB), lambda b, h, i: (h, b * nq + i, 0, 0)),
                  pl.BlockSpec((S, step * HEAD_DIM), lambda b, h, i: (b, h)),
                  pl.BlockSpec((step, nq, HEAD_DIM, TB), lambda b, h, i: (h, b, 0, 0))],
        out_specs=pl.BlockSpec((step * HEAD_DIM, TB), lambda b, h, i: (h, b * nq + i)),
        out_shape=jax.ShapeDtypeStruct((n_heads * HEAD_DIM, nblk * TB), BF16),
        scratch_shapes=[pltpu.VMEM((step, 1, TB), F32), pltpu.VMEM((step, HEAD_DIM, TB), F32),
                        pltpu.VMEM((step, TB, TB), F32), pltpu.VMEM((step, 2 * TB, TB), BF16),
                        pltpu.VMEM((step, TB, TB), BF16)],
        compiler_params=_params(3),
        name="stickbreak_attn",
    )(qt, k, vt)


def _moba_kernel(q_ref, k_ref, v_ref, bias_ref, o_ref, kmean_ref, mask_ref, s_ref, p_ref, *,
                 n_tiles):
    n_heads, nq = q_ref.shape[0], q_ref.shape[1]
    heads = range(n_heads)

    def key_block(hh, j):
        return k_ref[pl.ds(pl.multiple_of(j * TB, TB), TB), hh * PAIR:(hh + 1) * PAIR]

    def block_means(jb, _):
        for hh in heads:
            blk = key_block(hh, jb).astype(F32)
            kmean_ref[hh, pl.ds(jb, 1), :] = jnp.mean(blk, axis=0, keepdims=True)
        return 0

    lax.fori_loop(0, nq, block_means, 0)

    blk_idx = lax.broadcasted_iota(jnp.int32, (nq, TB), 0)

    def choose(i, _):
        for hh in heads:
            q_gate = jnp.concatenate([q_ref[hh, i], jnp.zeros((PAIR - HEAD_DIM, TB), BF16)],
                                     axis=0)
            km = kmean_ref[hh]
            km_hi = km.astype(BF16)
            km_lo = (km - km_hi.astype(F32)).astype(BF16)
            gate = (jnp.dot(km_hi, q_gate, preferred_element_type=F32)
                    + jnp.dot(km_lo, q_gate, preferred_element_type=F32))
            gate = jnp.where(blk_idx < i, gate, NEG_INF)
            chosen = blk_idx == i
            for _ in range(MOBA_TOPK):
                best = jnp.max(gate, axis=0, keepdims=True)
                cand = jnp.where(jnp.logical_and(gate == best, best > 0.5 * NEG_INF),
                                 blk_idx, nq)
                pick = blk_idx == jnp.min(cand, axis=0, keepdims=True)
                chosen = jnp.logical_or(chosen, pick)
                gate = jnp.where(pick, NEG_INF, gate)
            mask_ref[hh, i] = jnp.where(chosen, 0.0, NEG_INF)
        return 0

    def choose_some(t, _):
        for k in range(MOBA_UNROLL):
            choose(t * MOBA_UNROLL + k, 0)
        return 0

    lax.fori_loop(0, nq // MOBA_UNROLL, choose_some, 0)

    first_row = lax.broadcasted_iota(jnp.int32, (16, TB), 0) == 0
    ones_row = jnp.where(first_row, 1.0, 0.0).astype(BF16)
    q_tail = jnp.zeros((PAIR - HEAD_DIM - 16, TB), BF16)

    def following(item):
        i, g = item
        wrap = g == i // MOBA_UNROLL
        stay = jnp.logical_and(wrap, i == nq - 1)
        return (jnp.where(jnp.logical_and(wrap, jnp.logical_not(stay)), i + 1, i),
                jnp.where(stay, g, jnp.where(wrap, 0, g + 1)))

    def score_group(hh, item, slot):
        i, g = item
        q = q_ref[hh, i]
        raw = []
        for u in range(MOBA_UNROLL):
            j = g * MOBA_UNROLL + u
            mask_rows = jnp.where(first_row, mask_ref[hh, i, pl.ds(j, 1), :], 0.0).astype(BF16)
            q_ext = jnp.concatenate([q, mask_rows, q_tail], axis=0)
            raw.append(jnp.dot(key_block(hh, j), q_ext, preferred_element_type=F32))
        top = jnp.full((1, TB), NEG_INF, F32)
        for u in range(MOBA_UNROLL):
            tile = jnp.clip(i - (g * MOBA_UNROLL + u), 0, n_tiles - 1)
            s = raw[u] + bias_ref[hh, tile]
            s_ref[hh, slot, u] = s
            top = jnp.maximum(top, jnp.max(s, axis=0, keepdims=True))
        return top

    def value_group(hh, item, slot, top, state):
        i, g = item
        fresh = g == 0
        m = jnp.where(fresh, NEG_INF, state[0])
        acc = jnp.where(fresh, 0.0, state[1])
        m_new = jnp.maximum(m, top)
        for u in range(MOBA_UNROLL):
            p_ref[hh, slot, u] = jnp.exp2(s_ref[hh, slot, u] - m_new).astype(BF16)
        pv = None
        for u in range(MOBA_UNROLL):
            v_ext = jnp.concatenate([v_ref[hh, g * MOBA_UNROLL + u], ones_row], axis=0)
            part = jnp.dot(v_ext, p_ref[hh, slot, u], preferred_element_type=F32)
            pv = part if pv is None else pv + part
        acc = jnp.exp2(m - m_new) * acc + pv
        o_ref[hh, i] = (acc[:HEAD_DIM] / acc[HEAD_DIM:HEAD_DIM + 1]).astype(BF16)
        return m_new, acc

    def pair(carry):
        item_a, tops_a, states = carry
        item_b = following(item_a)
        item_c = following(item_b)
        tops_b = [score_group(hh, item_b, 1) for hh in heads]
        states = [value_group(hh, item_a, 0, tops_a[hh], states[hh]) for hh in heads]
        tops_c = [score_group(hh, item_c, 0) for hh in heads]
        states = [value_group(hh, item_b, 1, tops_b[hh], states[hh]) for hh in heads]
        return item_c, tops_c, states

    def trip(_, carry):
        for _ in range(MOBA_PAIRS_PER_TRIP):
            carry = pair(carry)
        return carry

    n_groups = nq // MOBA_UNROLL
    n_items = MOBA_UNROLL * n_groups * (n_groups + 1) // 2
    assert n_items % (2 * MOBA_PAIRS_PER_TRIP) == 0
    first = (jnp.int32(0), jnp.int32(0))
    init = [(jnp.full((1, TB), NEG_INF, F32), jnp.zeros((HEAD_DIM + 16, TB), F32))
            for _ in heads]
    lax.fori_loop(0, n_items // (2 * MOBA_PAIRS_PER_TRIP), trip,
                  (first, [score_group(hh, first, 0) for hh in heads], init))


def _moba_attention(qt, k, vt, bias_tiles, batch):
    n_heads, nblk = qt.shape[0], qt.shape[1]
    nq = nblk // batch
    S = nq * TB
    n_tiles = bias_tiles.shape[1]
    step = MOBA_HEADS_PER_STEP
    assert n_heads % step == 0 and nq % MOBA_UNROLL == 0
    return pl.pallas_call(
        functools.partial(_moba_kernel, n_tiles=n_tiles),
        grid=(n_heads // step, batch),
        in_specs=[pl.BlockSpec((step, nq, HEAD_DIM, TB), lambda h, b: (h, b, 0, 0)),
                  pl.BlockSpec((S, step * PAIR), lambda h, b: (b, h)),
                  pl.BlockSpec((step, nq, HEAD_DIM, TB), lambda h, b: (h, b, 0, 0)),
                  pl.BlockSpec((step, n_tiles, TB, TB), lambda h, b: (h, 0, 0, 0),
                               pipeline_mode=pl.Buffered(1))],
        out_specs=pl.BlockSpec((step, nq, HEAD_DIM, TB), lambda h, b: (h, b, 0, 0)),
        out_shape=jax.ShapeDtypeStruct((n_heads, nblk, HEAD_DIM, TB), BF16),
        scratch_shapes=[pltpu.VMEM((step, nq, PAIR), F32), pltpu.VMEM((step, nq, nq, TB), F32),
                        pltpu.VMEM((step, 2, MOBA_UNROLL, TB, TB), F32),
                        pltpu.VMEM((step, 2, MOBA_UNROLL, TB, TB), BF16)],
        compiler_params=_params(2),
        name="moba_attn",
    )(qt, k, vt, bias_tiles)


def _swa_kernel(sink_ref, q_ref, kp_ref, kc_ref, vp_ref, vc_ref, bias_ref, o_ref, s_ref, p_ref):
    kv_head = pl.program_id(0)
    i = pl.program_id(2)
    group = q_ref.shape[0]
    w = SWA_WINDOW
    kp, kc = kp_ref[...], kc_ref[...]
    vp, vc = vp_ref[0, 0], vc_ref[0, 0]
    keys = [jnp.concatenate([kp[TB - w:, :], kc[:w, :]], axis=0), kc]
    vals = [jnp.concatenate([vp[:, TB - w:], vc[:, :w]], axis=1), vc]
    first_rows_exist = jnp.logical_or(
        lax.broadcasted_iota(jnp.int32, (2 * w, w), 0) >= w, i > 0)
    n_halves = TB // w
    sinks = [sink_ref[kv_head * group + hh] * LOG2E for hh in range(group)]
    maxes, denoms = [], []
    for hh in range(group):
        q_ext = _pair_extend(q_ref[hh, 0], kv_head)
        for u in range(n_halves):
            s = (jnp.dot(keys[u], q_ext[:, u * w:(u + 1) * w], preferred_element_type=F32)
                 + bias_ref[hh])
            if u == 0:
                s = jnp.where(first_rows_exist, s, NEG_INF)
            s_ref[hh * n_halves + u] = s
            maxes.append(jnp.maximum(jnp.max(s, axis=0, keepdims=True), sinks[hh]))
    for c in range(group * n_halves):
        p = jnp.exp2(s_ref[c] - maxes[c])
        denoms.append(jnp.sum(p, axis=0, keepdims=True) + jnp.exp2(sinks[c // n_halves] - maxes[c]))
        p_ref[c] = p.astype(BF16)
    for hh in range(group):
        halves = [jnp.dot(vals[u], p_ref[hh * n_halves + u], preferred_element_type=F32)
                  / denoms[hh * n_halves + u] for u in range(n_halves)]
        o_ref[hh * HEAD_DIM:(hh + 1) * HEAD_DIM, :] = jnp.concatenate(halves, axis=1).astype(BF16)


def _swa_attention(qt, k, vt, bias_tiles, sinks, batch):
    n_heads, nblk = qt.shape[0], qt.shape[1]
    n_kv = vt.shape[0]
    group = n_heads // n_kv
    nq = nblk // batch
    assert TB == 2 * SWA_WINDOW

    def prev(b, i):
        return b * nq + jnp.maximum(i - 1, 0)

    return pl.pallas_call(
        _swa_kernel,
        grid_spec=pltpu.PrefetchScalarGridSpec(
            num_scalar_prefetch=1,
            grid=(n_kv, batch, nq),
            in_specs=[
                pl.BlockSpec((group, 1, HEAD_DIM, TB), lambda g, b, i, s: (g, b * nq + i, 0, 0)),
                pl.BlockSpec((TB, PAIR), lambda g, b, i, s: (prev(b, i), g // 2)),
                pl.BlockSpec((TB, PAIR), lambda g, b, i, s: (b * nq + i, g // 2)),
                pl.BlockSpec((1, 1, HEAD_DIM, TB), lambda g, b, i, s: (g, prev(b, i), 0, 0)),
                pl.BlockSpec((1, 1, HEAD_DIM, TB), lambda g, b, i, s: (g, b * nq + i, 0, 0)),
                pl.BlockSpec((group, 2 * SWA_WINDOW, SWA_WINDOW), lambda g, b, i, s: (g, 0, 0)),
            ],
            out_specs=pl.BlockSpec((group * HEAD_DIM, TB), lambda g, b, i, s: (g, b * nq + i)),
            scratch_shapes=[pltpu.VMEM((2 * group, 2 * SWA_WINDOW, SWA_WINDOW), F32),
                            pltpu.VMEM((2 * group, 2 * SWA_WINDOW, SWA_WINDOW), BF16)],
        ),
        out_shape=jax.ShapeDtypeStruct((n_heads * HEAD_DIM, nblk * TB), BF16),
        compiler_params=_params(3),
        name="swa_attn",
    )(sinks, qt, k, k, vt, vt, bias_tiles)


def _mem_kv_kernel(mem_ref, g_ref, wt_ref, kg_ref, k_ref, v_ref):
    h = _rms_rows(mem_ref[0], g_ref[0]).astype(BF16)
    t = lax.dot_general(wt_ref[0], h, _NT, preferred_element_type=F32)
    cw = CROSS_HEADS * CROSS_HEAD_DIM
    for hh in range(CROSS_HEADS):
        blk = t[hh * CROSS_HEAD_DIM:(hh + 1) * CROSS_HEAD_DIM, :]
        kn = blk * _rms_scale_cols(blk) * kg_ref[0]
        k_ref[0, 0, hh] = kn.T.astype(BF16)
        v_ref[0, 0, hh] = t[cw + hh * CROSS_HEAD_DIM:cw + (hh + 1) * CROSS_HEAD_DIM, :].astype(BF16)


def _mem_kv(mem, gains, wts, kgains):
    B, M, D = mem.shape
    depth = gains.shape[0]
    kshape = (depth, B, CROSS_HEADS, M, CROSS_HEAD_DIM)
    vshape = (depth, B, CROSS_HEADS, CROSS_HEAD_DIM, M)
    return pl.pallas_call(
        _mem_kv_kernel,
        grid=(depth, B),
        in_specs=[pl.BlockSpec((1, M, D), lambda l, b: (b, 0, 0)),
                  pl.BlockSpec((1, 1, D), lambda l, b: (l, 0, 0)),
                  pl.BlockSpec((1,) + wts.shape[1:], lambda l, b: (l, 0, 0)),
                  pl.BlockSpec((1,) + kgains.shape[1:], lambda l, b: (l, 0, 0))],
        out_specs=[pl.BlockSpec((1, 1) + kshape[2:], lambda l, b: (l, b, 0, 0, 0)),
                   pl.BlockSpec((1, 1) + vshape[2:], lambda l, b: (l, b, 0, 0, 0))],
        out_shape=[jax.ShapeDtypeStruct(kshape, BF16), jax.ShapeDtypeStruct(vshape, BF16)],
        compiler_params=_params(2),
        name="mem_kv",
    )(mem, gains, wts, kgains)


def _post_kernel(*refs, n_attn):
    x_ref = refs[0]
    attn_refs = refs[1:1 + n_attn]
    wout_refs = refs[1 + n_attn:1 + 2 * n_attn]
    g_ref, wq_ref, k_ref, v_ref, wo_ref, o_ref, h_ref, qn_ref, p_ref, ot_ref = refs[1 + 2 * n_attn:]
    subs = [slice(s * TB, (s + 1) * TB) for s in range(x_ref.shape[0] // TB)]
    hd = CROSS_HEAD_DIM
    for s, rows in enumerate(subs):
        x = x_ref[rows, :]
        for a_ref, w_ref in zip(attn_refs, wout_refs):
            if len(a_ref.shape) == 2:
                a = a_ref[:, rows]
            else:
                a = jnp.concatenate([a_ref[hh, s] for hh in range(a_ref.shape[0])], axis=0)
            a = a.astype(F32).T.astype(BF16)
            x = x + jnp.dot(a, w_ref[...], preferred_element_type=F32)
        o_ref[rows, :] = x
        h_ref[rows, :] = _rms_rows(x, g_ref[...]).astype(BF16)
    scale = hd ** -0.5 * LOG2E
    for cols in subs:
        qt = lax.dot_general(wq_ref[...], h_ref[cols, :], _NT, preferred_element_type=F32)
        for hh in range(CROSS_HEADS):
            blk = qt[hh * hd:(hh + 1) * hd, :]
            qn_ref[hh, :, cols] = (blk * (_rms_scale_cols(blk) * scale)).astype(BF16)
    denoms = {}
    for s, cols in enumerate(subs):
        for hh in range(CROSS_HEADS):
            sc = jnp.dot(k_ref[0, 0, hh], qn_ref[hh, :, cols], preferred_element_type=F32)
            p = jnp.exp2(sc - jnp.max(sc, axis=0, keepdims=True))
            denoms[s, hh] = jnp.sum(p, axis=0, keepdims=True)
            p_ref[hh, :, cols] = p.astype(BF16)
    for s, cols in enumerate(subs):
        for hh in range(CROSS_HEADS):
            o = jnp.dot(v_ref[0, 0, hh], p_ref[hh, :, cols], preferred_element_type=F32)
            ot_ref[hh * hd:(hh + 1) * hd, cols] = o / denoms[s, hh]
    for rows in subs:
        o_ref[rows, :] += jnp.dot(ot_ref[:, rows].T.astype(BF16), wo_ref[...],
                                  preferred_element_type=F32)


def _post_mixer(x, attn_list, wout_list, gain, wq_t, mem_k, mem_v, wo, layer, seq):
    T, D = x.shape
    tile = POST_TILE
    per_seq = seq // tile
    n_attn = len(attn_list)
    const = lambda i: (0, 0)
    in_specs = [pl.BlockSpec((tile, D), lambda i: (i, 0))]
    for a in attn_list:
        if a.ndim == 2:
            in_specs.append(pl.BlockSpec((a.shape[0], tile), lambda i: (0, i)))
        else:
            in_specs.append(pl.BlockSpec((a.shape[0], tile // TB) + a.shape[2:],
                                         lambda i: (0, i, 0, 0)))
    in_specs += [pl.BlockSpec(w.shape, const) for w in wout_list]
    in_specs += [pl.BlockSpec((1, D), const), pl.BlockSpec(wq_t.shape, const),
                 pl.BlockSpec((1, 1) + mem_k.shape[2:], lambda i: (layer, i // per_seq, 0, 0, 0)),
                 pl.BlockSpec((1, 1) + mem_v.shape[2:], lambda i: (layer, i // per_seq, 0, 0, 0)),
                 pl.BlockSpec(wo.shape, const)]
    return pl.pallas_call(
        functools.partial(_post_kernel, n_attn=n_attn),
        grid=(T // tile,),
        in_specs=in_specs,
        out_specs=pl.BlockSpec((tile, D), lambda i: (i, 0)),
        out_shape=jax.ShapeDtypeStruct((T, D), F32),
        scratch_shapes=[pltpu.VMEM((tile, D), BF16),
                        pltpu.VMEM((CROSS_HEADS, CROSS_HEAD_DIM, tile), BF16),
                        pltpu.VMEM((CROSS_HEADS, mem_k.shape[3], tile), BF16),
                        pltpu.VMEM((CROSS_HEADS * CROSS_HEAD_DIM, tile), F32)],
        compiler_params=_params(1),
        name="outproj_cross",
    )(x, *attn_list, *wout_list, gain, wq_t, mem_k, mem_v, wo)


def _ffn_kernel(x_ref, g_ref, w1_ref, cw_ref, w2_ref, o_ref, tail_ref, ext_ref, h_ref, act_ref, *,
                per_seq):
    i = pl.program_id(0)
    tile = x_ref.shape[0]
    n_chunks = w2_ref.shape[0]
    lead = 8
    x = x_ref[...]
    h_ref[...] = _rms_rows(x, g_ref[...]).astype(BF16)

    @pl.when(i % per_seq == 0)
    def _():
        tail_ref[...] = jnp.zeros(tail_ref.shape, F32)

    o_ref[...] = x

    def project(c, slot):
        for part in range(2):
            cc = c + part * n_chunks
            u = jnp.dot(h_ref[...], w1_ref[cc], preferred_element_type=F32)
            ext_ref[slot, part, 0:lead, :] = tail_ref[cc]
            ext_ref[slot, part, lead:, :] = u
            tail_ref[cc] = u[tile - lead:, :]

    def gate_up(c, slot):
        halves = []
        for part in range(2):
            cw = cw_ref[c + part * n_chunks]
            ext = ext_ref.at[slot, part]
            halves.append(cw[3:4, :] + cw[0:1, :] * ext[lead - 2:lead - 2 + tile, :]
                          + cw[1:2, :] * ext[lead - 1:lead - 1 + tile, :]
                          + cw[2:3, :] * ext[lead:, :])
        gate, up = halves
        act_ref[c] = (gate * (1.0 / (1.0 + jnp.exp(-gate))) * up).astype(BF16)

    def down(group):
        a = jnp.concatenate([act_ref[c] for c in group], axis=1)
        w = jnp.concatenate([w2_ref[c] for c in group], axis=0)
        o_ref[...] += jnp.dot(a, w, preferred_element_type=F32)

    groups = [list(range(s, min(s + FFN_GROUP, n_chunks))) for s in range(0, n_chunks, FFN_GROUP)]

    def slot(gi, k):
        return (gi % 2) * FFN_GROUP + k

    for k, c in enumerate(groups[0]):
        project(c, slot(0, k))
    for gi, group in enumerate(groups):
        if gi + 1 < len(groups):
            for k, c in enumerate(groups[gi + 1]):
                project(c, slot(gi + 1, k))
        for k, c in enumerate(group):
            gate_up(c, slot(gi, k))
        down(group)


def _conv_ffn(x, gain, w1, cw, w2, seq):
    T, D = x.shape
    tile = FFN_TILE
    const3 = lambda i: (0, 0, 0)
    return pl.pallas_call(
        functools.partial(_ffn_kernel, per_seq=seq // tile),
        grid=(T // tile,),
        in_specs=[pl.BlockSpec((tile, D), lambda i: (i, 0)),
                  pl.BlockSpec((1, D), lambda i: (0, 0)),
                  pl.BlockSpec(w1.shape, const3, pipeline_mode=pl.Buffered(1)),
                  pl.BlockSpec(cw.shape, const3),
                  pl.BlockSpec(w2.shape, const3, pipeline_mode=pl.Buffered(1))],
        out_specs=pl.BlockSpec((tile, D), lambda i: (i, 0)),
        out_shape=jax.ShapeDtypeStruct((T, D), F32),
        scratch_shapes=[pltpu.VMEM((w1.shape[0], 8, FFN_CHUNK), F32),
                        pltpu.VMEM((2 * FFN_GROUP, 2, tile + 8, FFN_CHUNK), F32),
                        pltpu.VMEM((tile, D), BF16),
                        pltpu.VMEM((w2.shape[0], tile, FFN_CHUNK), BF16)],
        compiler_params=_params(1),
        name="conv_ffn",
    )(x, gain, w1, cw, w2)


def _t5_bucket(dist):
    n = jnp.maximum(dist, 0)
    max_exact = NUM_BUCKETS // 2
    nf = jnp.maximum(n, 1).astype(F32)
    coef = (NUM_BUCKETS - max_exact) / math.log(MAX_DISTANCE / max_exact)
    large = max_exact + (jnp.log(nf / max_exact) * coef).astype(jnp.int32)
    large = jnp.minimum(large, NUM_BUCKETS - 1)
    return jnp.where(n < max_exact, n, large)


def _toeplitz_kernel(w_ref, o_ref, *, window):
    n_tiles, rows, cols = o_ref.shape[1:]
    period = w_ref.shape[-1]
    key = lax.broadcasted_iota(jnp.int32, (rows, cols), 0)
    qry = lax.broadcasted_iota(jnp.int32, (rows, cols), 1)
    for d in range(n_tiles):
        spread = jnp.broadcast_to(w_ref[0, d], (rows, period))
        skew = pltpu.roll(spread, 0, 1, stride=1, stride_axis=0)
        tile = skew[:, rows:rows + cols]
        if window is not None:
            tile = jnp.where(jnp.logical_and(qry - key + window >= 0, qry - key < 0), tile, NEG_INF)
        elif d == 0:
            tile = jnp.where(qry >= key, tile, NEG_INF)
        o_ref[0, d] = tile


def _toeplitz_tiles(windows, rows, cols, window):
    n_heads, n_tiles, _, period = windows.shape
    assert period >= rows + cols and rows % 128 == 0 and cols % 128 == 0 and period % 128 == 0
    return pl.pallas_call(
        functools.partial(_toeplitz_kernel, window=window),
        grid=(n_heads,),
        in_specs=[pl.BlockSpec((1, n_tiles, 1, period), lambda h: (h, 0, 0, 0))],
        out_specs=pl.BlockSpec((1, n_tiles, rows, cols), lambda h: (h, 0, 0, 0)),
        out_shape=jax.ShapeDtypeStruct((n_heads, n_tiles, rows, cols), F32),
        compiler_params=_params(1),
        name="bias_tiles",
    )(windows)


def _moba_bias_tiles(bias_by_dist):
    n_tiles = min(bias_by_dist.shape[1] // TB, -(-(MAX_DISTANCE + TB) // TB))
    period = 3 * TB
    padded = jnp.pad(bias_by_dist, ((0, 0), (TB, period)))
    windows = jnp.stack([padded[:, d * TB:d * TB + period] for d in range(n_tiles)], axis=1)
    return _toeplitz_tiles(windows[:, :, None, :], TB, TB, None)


def _swa_bias_tiles(bias_by_dist):
    w = SWA_WINDOW
    windows = jnp.pad(bias_by_dist[:, :3 * w], ((0, 0), (w, 0)))
    return _toeplitz_tiles(windows[:, None, None, :], 2 * w, w, w)[:, 0]


def kernel(x, mem, rel_bias, mix_norm, ev_w_in, ev_w_out, ev_q_gain, ev_k_gain, od_w_in, od_w_out, od_q_gain, od_k_gain, od_sinks, cx_norm, cx_mem_norm, cx_w_q, cx_w_kv, cx_w_o, cx_q_gain, cx_k_gain, ff_norm, ff_w_in, ff_conv_w, ff_conv_b, ff_w_out):
    B, S, D = x.shape
    T = B * S
    depth = mix_norm.shape[0]
    n_heads = rel_bias.shape[1]
    sb_heads = n_heads // 2
    d_ff = ff_w_out.shape[1]
    n_chunks = d_ff // FFN_CHUNK
    n_kv = (od_w_in.shape[2] // HEAD_DIM - n_heads) // 2
    assert S % FFN_TILE == 0 and S % POST_TILE == 0 and S % TB == 0 and d_ff % FFN_CHUNK == 0

    def lane_bcast(col, width):
        return jnp.broadcast_to(col[:, None], (col.shape[0], width)).astype(F32)

    bias_by_dist = rel_bias.astype(F32)[_t5_bucket(jnp.arange(S + TB, dtype=jnp.int32))].T
    moba_tiles = _moba_bias_tiles(bias_by_dist[sb_heads:, :S] * LOG2E)
    swa_tiles = _swa_bias_tiles(bias_by_dist * LOG2E)

    cx_kgain = jnp.stack([lane_bcast(cx_k_gain[l] * cx_q_gain[l], mem.shape[1]) for l in range(depth)])
    mem_k, mem_v = _mem_kv(mem, cx_mem_norm[:, None, :], jnp.swapaxes(cx_w_kv, 1, 2).astype(BF16),
                           cx_kgain)

    xf = x.reshape(T, D)
    for layer in range(depth):
        idx = layer // 2
        gain = mix_norm[layer][None, :]
        if layer % 2 == 0:
            kgain = lane_bcast(jnp.tile(ev_k_gain[idx] * ev_q_gain[idx], sb_heads), TB)
            qa, ka, va, qb, kb, vb = _even_inproj(xf, gain, ev_w_in[idx].T.astype(BF16), kgain)
            attn = [_sb_attention(qa, ka, va, B), _moba_attention(qb, kb, vb, moba_tiles, B)]
            half = sb_heads * HEAD_DIM
            wout = [ev_w_out[idx, :half].astype(BF16), ev_w_out[idx, half:].astype(BF16)]
        else:
            kgain = lane_bcast(jnp.tile(od_k_gain[idx] * od_q_gain[idx], n_kv), TB)
            q, k, v = _odd_inproj(xf, gain, od_w_in[idx].T.astype(BF16), kgain, n_heads, n_kv)
            attn = [_swa_attention(q, k, v, swa_tiles, od_sinks[idx].astype(F32), B)]
            wout = [od_w_out[idx].astype(BF16)]
        xf = _post_mixer(xf, attn, wout, cx_norm[layer][None, :], cx_w_q[layer].T.astype(BF16),
                         mem_k, mem_v, cx_w_o[layer].astype(BF16), layer, S)
        w1 = ff_w_in[layer].reshape(D, 2 * n_chunks, FFN_CHUNK).transpose(1, 0, 2).astype(BF16)
        cw = jnp.concatenate([ff_conv_w[layer], ff_conv_b[layer][None, :]], axis=0)
        cw = cw.reshape(CONV_WIDTH + 1, 2 * n_chunks, FFN_CHUNK).transpose(1, 0, 2)
        w2 = ff_w_out[layer].reshape(n_chunks, FFN_CHUNK, D).astype(BF16)
        xf = _conv_ffn(xf, ff_norm[layer][None, :], w1, cw, w2, S)
    return xf.reshape(B, S, D)
```

```python
import functools
import math

import jax
import jax.numpy as jnp
import numpy as np
from jax import lax
from jax.experimental import pallas as pl
from jax.experimental.pallas import tpu as pltpu

F32 = jnp.float32
BF16 = jnp.bfloat16

HEAD_DIM = 64
PAIR = 2 * HEAD_DIM
TB = 256
SWA_WINDOW = 128
MOBA_TOPK = 3
MOBA_UNROLL = 4
MOBA_HEADS_PER_STEP = 2
MOBA_PAIRS_PER_TRIP = 2
LOG2E = math.log2(math.e)
NUM_BUCKETS = 32
MAX_DISTANCE = 4096
CROSS_HEADS = 4
CROSS_HEAD_DIM = 128
CONV_WIDTH = 3
RMS_EPS = 1e-6
NEG_INF = -1e30
SB_DEAD_LOG2 = -150.5
SB_HEADS_PER_STEP = 8
FFN_CHUNK = 256
FFN_TILE = 512
FFN_GROUP = 6
POST_TILE = 512
VMEM_LIMIT = 56 * 1024 * 1024

_NT = (((1,), (1,)), ((), ()))


def _rms_rows(x, gain):
    ms = jnp.mean(x * x, axis=-1, keepdims=True)
    return (x * lax.rsqrt(ms + RMS_EPS)) * gain


def _rms_scale_cols(xt):
    return lax.rsqrt(jnp.mean(xt * xt, axis=0, keepdims=True) + RMS_EPS)


def _params(n_axes):
    return pltpu.CompilerParams(dimension_semantics=("arbitrary",) * n_axes,
                                vmem_limit_bytes=VMEM_LIMIT)


def _pair_extend(q, head):
    odd = head % 2 == 1
    zero = jnp.zeros_like(q)
    return jnp.concatenate([jnp.where(odd, zero, q), jnp.where(odd, q, zero)], axis=0)


def _even_inproj_kernel(x_ref, g_ref, wt_ref, kg_ref,
                        qa_ref, ka_ref, va_ref, qb_ref, kb_ref, vb_ref):
    h = _rms_rows(x_ref[...], g_ref[...]).astype(BF16)
    width = 8 * HEAD_DIM

    def proj(group):
        w = wt_ref[group * width:(group + 1) * width, :]
        return lax.dot_general(w, h, _NT, preferred_element_type=F32)

    def heads(t):
        return [t[hh * HEAD_DIM:(hh + 1) * HEAD_DIM, :] for hh in range(8)]

    scale = HEAD_DIM ** -0.5
    for hh, blk in enumerate(heads(proj(0))):
        qa_ref[hh, 0] = (blk * (scale * LOG2E)).astype(BF16)
    ka_ref[...] = proj(1).T.astype(BF16)
    for hh, blk in enumerate(heads(proj(2))):
        va_ref[hh, 0] = blk.astype(BF16)
    for hh, blk in enumerate(heads(proj(3))):
        qb_ref[hh, 0] = (blk * (_rms_scale_cols(blk) * (scale * LOG2E))).astype(BF16)
    one_then_zeros = jnp.where(lax.broadcasted_iota(jnp.int32, (HEAD_DIM, TB), 0) == 0, 1.0, 0.0)
    kg = kg_ref[...]
    kb = []
    for hh, blk in enumerate(heads(proj(4))):
        kb += [blk * _rms_scale_cols(blk) * kg[hh * HEAD_DIM:(hh + 1) * HEAD_DIM, :], one_then_zeros]
    kb_ref[...] = jnp.concatenate(kb, axis=0).T.astype(BF16)
    for hh, blk in enumerate(heads(proj(5))):
        vb_ref[hh, 0] = blk.astype(BF16)


def _even_inproj(x, gain, wt, kgain):
    T, D = x.shape
    nblk = T // TB
    blocked = jax.ShapeDtypeStruct((8, nblk, HEAD_DIM, TB), BF16)
    rowmajor = jax.ShapeDtypeStruct((T, 8 * HEAD_DIM), BF16)
    bspec = pl.BlockSpec((8, 1, HEAD_DIM, TB), lambda i: (0, i, 0, 0))
    rspec = pl.BlockSpec((TB, 8 * HEAD_DIM), lambda i: (i, 0))
    return pl.pallas_call(
        _even_inproj_kernel,
        grid=(nblk,),
        in_specs=[pl.BlockSpec((TB, D), lambda i: (i, 0)),
                  pl.BlockSpec((1, D), lambda i: (0, 0)),
                  pl.BlockSpec(wt.shape, lambda i: (0, 0)),
                  pl.BlockSpec(kgain.shape, lambda i: (0, 0))],
        out_specs=[bspec, rspec, bspec, bspec, pl.BlockSpec((TB, 8 * PAIR), lambda i: (i, 0)), bspec],
        out_shape=[blocked, rowmajor, blocked, blocked,
                   jax.ShapeDtypeStruct((T, 8 * PAIR), BF16), blocked],
        compiler_params=_params(1),
        name="even_inproj",
    )(x, gain, wt, kgain)


def _odd_inproj_kernel(x_ref, g_ref, wt_ref, kg_ref, q_ref, k_ref, v_ref, *, n_q, n_kv):
    h = _rms_rows(x_ref[...], g_ref[...]).astype(BF16)
    scale = HEAD_DIM ** -0.5 * LOG2E
    group = 8 * HEAD_DIM
    for g in range(n_q * HEAD_DIM // group):
        w = wt_ref[g * group:(g + 1) * group, :]
        t = lax.dot_general(w, h, _NT, preferred_element_type=F32)
        for hh in range(8):
            blk = t[hh * HEAD_DIM:(hh + 1) * HEAD_DIM, :]
            q_ref[g * 8 + hh, 0] = (blk * (_rms_scale_cols(blk) * scale)).astype(BF16)
    kv_rows = n_kv * HEAD_DIM
    w = wt_ref[n_q * HEAD_DIM:n_q * HEAD_DIM + 2 * kv_rows, :]
    t = lax.dot_general(w, h, _NT, preferred_element_type=F32)
    kparts = []
    for hh in range(n_kv):
        blk = t[hh * HEAD_DIM:(hh + 1) * HEAD_DIM, :]
        kparts.append(blk * _rms_scale_cols(blk))
        v_ref[hh, 0] = t[kv_rows + hh * HEAD_DIM:kv_rows + (hh + 1) * HEAD_DIM, :].astype(BF16)
    k_ref[...] = (jnp.concatenate(kparts, axis=0) * kg_ref[...]).T.astype(BF16)


def _odd_inproj(x, gain, wt, kgain, n_q, n_kv):
    T, D = x.shape
    nblk = T // TB
    return pl.pallas_call(
        functools.partial(_odd_inproj_kernel, n_q=n_q, n_kv=n_kv),
        grid=(nblk,),
        in_specs=[pl.BlockSpec((TB, D), lambda i: (i, 0)),
                  pl.BlockSpec((1, D), lambda i: (0, 0)),
                  pl.BlockSpec(wt.shape, lambda i: (0, 0)),
                  pl.BlockSpec(kgain.shape, lambda i: (0, 0))],
        out_specs=[pl.BlockSpec((n_q, 1, HEAD_DIM, TB), lambda i: (0, i, 0, 0)),
                   pl.BlockSpec((TB, n_kv * HEAD_DIM), lambda i: (i, 0)),
                   pl.BlockSpec((n_kv, 1, HEAD_DIM, TB), lambda i: (0, i, 0, 0))],
        out_shape=[jax.ShapeDtypeStruct((n_q, nblk, HEAD_DIM, TB), BF16),
                   jax.ShapeDtypeStruct((T, n_kv * HEAD_DIM), BF16),
                   jax.ShapeDtypeStruct((n_kv, nblk, HEAD_DIM, TB), BF16)],
        compiler_params=_params(1),
        name="odd_inproj",
    )(x, gain, wt, kgain)


def _sb_kernel(q_ref, k_ref, v_ref, o_ref, carry_ref, acc_ref, lb_ref, hl_ref, w_ref):
    i = pl.program_id(2)
    n_heads = q_ref.shape[0]
    key_idx = lax.broadcasted_iota(jnp.int32, (TB, TB), 0)
    qry_idx = lax.broadcasted_iota(jnp.int32, (TB, TB), 1)
    later = jnp.where(qry_idx > key_idx, 1.0, 0.0).astype(BF16)
    later2 = jnp.concatenate([later, later], axis=1)
    past = key_idx < qry_idx
    zeros = jnp.zeros((HEAD_DIM, TB), BF16)

    def group(j, diagonal):
        rows = pl.ds(pl.multiple_of(j * TB, TB), TB)
        colsums = []
        for hh in range(n_heads):
            q = q_ref[hh, 0]
            q_ext = jnp.concatenate([q, zeros] if hh % 2 == 0 else [zeros, q], axis=0)
            kj = k_ref[rows, (hh // 2) * PAIR:(hh // 2 + 1) * PAIR]
            z = jnp.dot(kj, q_ext, preferred_element_type=F32)
            log_beta = jnp.minimum(z, 0.0) - jnp.log2(1.0 + jnp.exp2(-jnp.abs(z)))
            log_not = log_beta - z
            if diagonal:
                log_not = jnp.where(past, log_not, 0.0)
            hi = log_not.astype(BF16)
            lb_ref[hh] = log_beta
            hl_ref[hh, :TB, :] = hi
            hl_ref[hh, TB:, :] = (log_not - hi.astype(F32)).astype(BF16)
            colsums.append(jnp.sum(log_not, axis=0, keepdims=True))
        carries = []
        for hh in range(n_heads):
            between = jnp.dot(later2, hl_ref[hh], preferred_element_type=F32)
            carry = 0.0 if diagonal else carry_ref[hh]
            w = jnp.exp2(lb_ref[hh] + between + carry)
            if diagonal:
                w = jnp.where(past, w, 0.0)
            w_ref[hh] = w.astype(BF16)
            carries.append(carry + colsums[hh])
            carry_ref[hh] = carries[hh]
        for hh in range(n_heads):
            pv = jnp.dot(v_ref[hh, j], w_ref[hh], preferred_element_type=F32)
            acc_ref[hh] = pv if diagonal else acc_ref[hh] + pv
        return carries

    def alive_flag(carries):
        top = functools.reduce(jnp.maximum, carries)
        return (jnp.max(top) > SB_DEAD_LOG2).astype(jnp.int32)

    alive = alive_flag(group(i, True))

    def cond(state):
        return jnp.logical_and(state[0] >= 0, state[1] > 0)

    def body(state):
        j = state[0]
        return j - 1, alive_flag(group(j, False))

    lax.while_loop(cond, body, (i - 1, alive))
    for hh in range(n_heads):
        o_ref[hh * HEAD_DIM:(hh + 1) * HEAD_DIM, :] = acc_ref[hh].astype(BF16)


def _sb_attention(qt, k, vt, batch):
    n_heads, nblk = qt.shape[0], qt.shape[1]
    nq = nblk // batch
    S = nq * TB
    step = SB_HEADS_PER_STEP
    assert n_heads % step == 0 and step % 2 == 0
    return pl.pallas_call(
        _sb_kernel,
        grid=(batch, n_heads // step, nq),
        in_specs=[pl.BlockSpec((step, 1, HEAD_DIM, TB), lambda b, h, i: (h, b * nq + i, 0, 0)),
                  pl.BlockSpec((S, step * HEAD_DIM), lambda b, h, i: (b, h)),
                  pl.BlockSpec((step, nq, HEAD_DIM, TB), lambda b, h, i: (h, b, 0, 0))],
        out_specs=pl.BlockSpec((step * HEAD_DIM, TB), lambda b, h, i: (h, b * nq + i)),
        out_shape=jax.ShapeDtypeStruct((n_heads * HEAD_DIM, nblk * TB), BF16),
        scratch_shapes=[pltpu.VMEM((step, 1, TB), F32), pltpu.VMEM((step, HEAD_DIM, TB), F32),
                        pltpu.VMEM((step, TB, TB), F32), pltpu.VMEM((step, 2 * TB, TB), BF16),
                        pltpu.VMEM((step, TB, TB), BF16)],
        compiler_params=_params(3),
        name="stickbreak_attn",
    )(qt, k, vt)


def _moba_kernel(q_ref, k_ref, v_ref, bias_ref, o_ref, kmean_ref, mask_ref, s_ref, p_ref, *,
                 n_tiles):
    n_heads, nq = q_ref.shape[0], q_ref.shape[1]
    heads = range(n_heads)

    def key_block(hh, j):
        return k_ref[pl.ds(pl.multiple_of(j * TB, TB), TB), hh * PAIR:(hh + 1) * PAIR]

    def block_means(jb, _):
        for hh in heads:
            blk = key_block(hh, jb).astype(F32)
            kmean_ref[hh, pl.ds(jb, 1), :] = jnp.mean(blk, axis=0, keepdims=True)
        return 0

    lax.fori_loop(0, nq, block_means, 0)

    blk_idx = lax.broadcasted_iota(jnp.int32, (nq, TB), 0)

    def choose(i, _):
        for hh in heads:
            q_gate = jnp.concatenate([q_ref[hh, i], jnp.zeros((PAIR - HEAD_DIM, TB), BF16)],
                                     axis=0)
            km = kmean_ref[hh]
            km_hi = km.astype(BF16)
            km_lo = (km - km_hi.astype(F32)).astype(BF16)
            gate = (jnp.dot(km_hi, q_gate, preferred_element_type=F32)
                    + jnp.dot(km_lo, q_gate, preferred_element_type=F32))
            gate = jnp.where(blk_idx < i, gate, NEG_INF)
            chosen = blk_idx == i
            for _ in range(MOBA_TOPK):
                best = jnp.max(gate, axis=0, keepdims=True)
                cand = jnp.where(jnp.logical_and(gate == best, best > 0.5 * NEG_INF),
                                 blk_idx, nq)
                pick = blk_idx == jnp.min(cand, axis=0, keepdims=True)
                chosen = jnp.logical_or(chosen, pick)
                gate = jnp.where(pick, NEG_INF, gate)
            mask_ref[hh, i] = jnp.where(chosen, 0.0, NEG_INF)
        return 0

    def choose_some(t, _):
        for k in range(MOBA_UNROLL):
            choose(t * MOBA_UNROLL + k, 0)
        return 0

    lax.fori_loop(0, nq // MOBA_UNROLL, choose_some, 0)

    first_row = lax.broadcasted_iota(jnp.int32, (16, TB), 0) == 0
    ones_row = jnp.where(first_row, 1.0, 0.0).astype(BF16)
    q_tail = jnp.zeros((PAIR - HEAD_DIM - 16, TB), BF16)

    def following(item):
        i, g = item
        wrap = g == i // MOBA_UNROLL
        stay = jnp.logical_and(wrap, i == nq - 1)
        return (jnp.where(jnp.logical_and(wrap, jnp.logical_not(stay)), i + 1, i),
                jnp.where(stay, g, jnp.where(wrap, 0, g + 1)))

    def score_group(hh, item, slot):
        i, g = item
        q = q_ref[hh, i]
        raw = []
        for u in range(MOBA_UNROLL):
            j = g * MOBA_UNROLL + u
            mask_rows = jnp.where(first_row, mask_ref[hh, i, pl.ds(j, 1), :], 0.0).astype(BF16)
            q_ext = jnp.concatenate([q, mask_rows, q_tail], axis=0)
            raw.append(jnp.dot(key_block(hh, j), q_ext, preferred_element_type=F32))
        top = jnp.full((1, TB), NEG_INF, F32)
        for u in range(MOBA_UNROLL):
            tile = jnp.clip(i - (g * MOBA_UNROLL + u), 0, n_tiles - 1)
            s = raw[u] + bias_ref[hh, tile]
            s_ref[hh, slot, u] = s
            top = jnp.maximum(top, jnp.max(s, axis=0, keepdims=True))
        return top

    def value_group(hh, item, slot, top, state):
        i, g = item
        fresh = g == 0
        m = jnp.where(fresh, NEG_INF, state[0])
        acc = jnp.where(fresh, 0.0, state[1])
        m_new = jnp.maximum(m, top)
        for u in range(MOBA_UNROLL):
            p_ref[hh, slot, u] = jnp.exp2(s_ref[hh, slot, u] - m_new).astype(BF16)
        pv = None
        for u in range(MOBA_UNROLL):
            v_ext = jnp.concatenate([v_ref[hh, g * MOBA_UNROLL + u], ones_row], axis=0)
            part = jnp.dot(v_ext, p_ref[hh, slot, u], preferred_element_type=F32)
            pv = part if pv is None else pv + part
        acc = jnp.exp2(m - m_new) * acc + pv
        o_ref[hh, i] = (acc[:HEAD_DIM] / acc[HEAD_DIM:HEAD_DIM + 1]).astype(BF16)
        return m_new, acc

    def pair(carry):
        item_a, tops_a, states = carry
        item_b = following(item_a)
        item_c = following(item_b)
        tops_b = [score_group(hh, item_b, 1) for hh in heads]
        states = [value_group(hh, item_a, 0, tops_a[hh], states[hh]) for hh in heads]
        tops_c = [score_group(hh, item_c, 0) for hh in heads]
        states = [value_group(hh, item_b, 1, tops_b[hh], states[hh]) for hh in heads]
        return item_c, tops_c, states

    def trip(_, carry):
        for _ in range(MOBA_PAIRS_PER_TRIP):
            carry = pair(carry)
        return carry

    n_groups = nq // MOBA_UNROLL
    n_items = MOBA_UNROLL * n_groups * (n_groups + 1) // 2
    assert n_items % (2 * MOBA_PAIRS_PER_TRIP) == 0
    first = (jnp.int32(0), jnp.int32(0))
    init = [(jnp.full((1, TB), NEG_INF, F32), jnp.zeros((HEAD_DIM + 16, TB), F32))
            for _ in heads]
    lax.fori_loop(0, n_items // (2 * MOBA_PAIRS_PER_TRIP), trip,
                  (first, [score_group(hh, first, 0) for hh in heads], init))


def _moba_attention(qt, k, vt, bias_tiles, batch):
    n_heads, nblk = qt.shape[0], qt.shape[1]
    nq = nblk // batch
    S = nq * TB
    n_tiles = bias_tiles.shape[1]
    step = MOBA_HEADS_PER_STEP
    assert n_heads % step == 0 and nq % MOBA_UNROLL == 0
    return pl.pallas_call(
        functools.partial(_moba_kernel, n_tiles=n_tiles),
        grid=(n_heads // step, batch),
        in_specs=[pl.BlockSpec((step, nq, HEAD_DIM, TB), lambda h, b: (h, b, 0, 0)),
                  pl.BlockSpec((S, step * PAIR), lambda h, b: (b, h)),
                  pl.BlockSpec((step, nq, HEAD_DIM, TB), lambda h, b: (h, b, 0, 0)),
                  pl.BlockSpec((step, n_tiles, TB, TB), lambda h, b: (h, 0, 0, 0),
                               pipeline_mode=pl.Buffered(1))],
        out_specs=pl.BlockSpec((step, nq, HEAD_DIM, TB), lambda h, b: (h, b, 0, 0)),
        out_shape=jax.ShapeDtypeStruct((n_heads, nblk, HEAD_DIM, TB), BF16),
        scratch_shapes=[pltpu.VMEM((step, nq, PAIR), F32), pltpu.VMEM((step, nq, nq, TB), F32),
                        pltpu.VMEM((step, 2, MOBA_UNROLL, TB, TB), F32),
                        pltpu.VMEM((step, 2, MOBA_UNROLL, TB, TB), BF16)],
        compiler_params=_params(2),
        name="moba_attn",
    )(qt, k, vt, bias_tiles)


def _swa_kernel(sink_ref, q_ref, kp_ref, kc_ref, vp_ref, vc_ref, bias_ref, o_ref, s_ref, p_ref):
    kv_head = pl.program_id(0)
    i = pl.program_id(2)
    group = q_ref.shape[0]
    w = SWA_WINDOW
    kp, kc = kp_ref[...], kc_ref[...]
    vp, vc = vp_ref[0, 0], vc_ref[0, 0]
    keys = [jnp.concatenate([kp[TB - w:, :], kc[:w, :]], axis=0), kc]
    vals = [jnp.concatenate([vp[:, TB - w:], vc[:, :w]], axis=1), vc]
    first_rows_exist = jnp.logical_or(
        lax.broadcasted_iota(jnp.int32, (2 * w, w), 0) >= w, i > 0)
    n_halves = TB // w
    sinks = [sink_ref[kv_head * group + hh] * LOG2E for hh in range(group)]
    maxes, denoms = [], []
    for hh in range(group):
        q_ext = _pair_extend(q_ref[hh, 0], kv_head)
        for u in range(n_halves):
            s = (jnp.dot(keys[u], q_ext[:, u * w:(u + 1) * w], preferred_element_type=F32)
                 + bias_ref[hh])
            if u == 0:
                s = jnp.where(first_rows_exist, s, NEG_INF)
            s_ref[hh * n_halves + u] = s
            maxes.append(jnp.maximum(jnp.max(s, axis=0, keepdims=True), sinks[hh]))
    for c in range(group * n_halves):
        p = jnp.exp2(s_ref[c] - maxes[c])
        denoms.append(jnp.sum(p, axis=0, keepdims=True) + jnp.exp2(sinks[c // n_halves] - maxes[c]))
        p_ref[c] = p.astype(BF16)
    for hh in range(group):
        halves = [jnp.dot(vals[u], p_ref[hh * n_halves + u], preferred_element_type=F32)
                  / denoms[hh * n_halves + u] for u in range(n_halves)]
        o_ref[hh * HEAD_DIM:(hh + 1) * HEAD_DIM, :] = jnp.concatenate(halves, axis=1).astype(BF16)


def _swa_attention(qt, k, vt, bias_tiles, sinks, batch):
    n_heads, nblk = qt.shape[0], qt.shape[1]
    n_kv = vt.shape[0]
    group = n_heads // n_kv
    nq = nblk // batch
    assert TB == 2 * SWA_WINDOW

    def prev(b, i):
        return b * nq + jnp.maximum(i - 1, 0)

    return pl.pallas_call(
        _swa_kernel,
        grid_spec=pltpu.PrefetchScalarGridSpec(
            num_scalar_prefetch=1,
            grid=(n_kv, batch, nq),
            in_specs=[
                pl.BlockSpec((group, 1, HEAD_DIM, TB), lambda g, b, i, s: (g, b * nq + i, 0, 0)),
                pl.BlockSpec((TB, PAIR), lambda g, b, i, s: (prev(b, i), g // 2)),
                pl.BlockSpec((TB, PAIR), lambda g, b, i, s: (b * nq + i, g // 2)),
                pl.BlockSpec((1, 1, HEAD_DIM, TB), lambda g, b, i, s: (g, prev(b, i), 0, 0)),
                pl.BlockSpec((1, 1, HEAD_DIM, TB), lambda g, b, i, s: (g, b * nq + i, 0, 0)),
                pl.BlockSpec((group, 2 * SWA_WINDOW, SWA_WINDOW), lambda g, b, i, s: (g, 0, 0)),
            ],
            out_specs=pl.BlockSpec((group * HEAD_DIM, TB), lambda g, b, i, s: (g, b * nq + i)),
            scratch_shapes=[pltpu.VMEM((2 * group, 2 * SWA_WINDOW, SWA_WINDOW), F32),
                            pltpu.VMEM((2 * group, 2 * SWA_WINDOW, SWA_WINDOW), BF16)],
        ),
        out_shape=jax.ShapeDtypeStruct((n_heads * HEAD_DIM, nblk * TB), BF16),
        compiler_params=_params(3),
        name="swa_attn",
    )(sinks, qt, k, k, vt, vt, bias_tiles)


def _mem_kv_kernel(mem_ref, g_ref, wt_ref, kg_ref, k_ref, v_ref):
    h = _rms_rows(mem_ref[0], g_ref[0]).astype(BF16)
    t = lax.dot_general(wt_ref[0], h, _NT, preferred_element_type=F32)
    cw = CROSS_HEADS * CROSS_HEAD_DIM
    for hh in range(CROSS_HEADS):
        blk = t[hh * CROSS_HEAD_DIM:(hh + 1) * CROSS_HEAD_DIM, :]
        kn = blk * _rms_scale_cols(blk) * kg_ref[0]
        k_ref[0, 0, hh] = kn.T.astype(BF16)
        v_ref[0, 0, hh] = t[cw + hh * CROSS_HEAD_DIM:cw + (hh + 1) * CROSS_HEAD_DIM, :].astype(BF16)


def _mem_kv(mem, gains, wts, kgains):
    B, M, D = mem.shape
    depth = gains.shape[0]
    kshape = (depth, B, CROSS_HEADS, M, CROSS_HEAD_DIM)
    vshape = (depth, B, CROSS_HEADS, CROSS_HEAD_DIM, M)
    return pl.pallas_call(
        _mem_kv_kernel,
        grid=(depth, B),
        in_specs=[pl.BlockSpec((1, M, D), lambda l, b: (b, 0, 0)),
                  pl.BlockSpec((1, 1, D), lambda l, b: (l, 0, 0)),
                  pl.BlockSpec((1,) + wts.shape[1:], lambda l, b: (l, 0, 0)),
                  pl.BlockSpec((1,) + kgains.shape[1:], lambda l, b: (l, 0, 0))],
        out_specs=[pl.BlockSpec((1, 1) + kshape[2:], lambda l, b: (l, b, 0, 0, 0)),
                   pl.BlockSpec((1, 1) + vshape[2:], lambda l, b: (l, b, 0, 0, 0))],
        out_shape=[jax.ShapeDtypeStruct(kshape, BF16), jax.ShapeDtypeStruct(vshape, BF16)],
        compiler_params=_params(2),
        name="mem_kv",
    )(mem, gains, wts, kgains)


def _post_kernel(*refs, n_attn):
    x_ref = refs[0]
    attn_refs = refs[1:1 + n_attn]
    wout_refs = refs[1 + n_attn:1 + 2 * n_attn]
    g_ref, wq_ref, k_ref, v_ref, wo_ref, o_ref, h_ref, qn_ref, p_ref, ot_ref = refs[1 + 2 * n_attn:]
    subs = [slice(s * TB, (s + 1) * TB) for s in range(x_ref.shape[0] // TB)]
    hd = CROSS_HEAD_DIM
    for s, rows in enumerate(subs):
        x = x_ref[rows, :]
        for a_ref, w_ref in zip(attn_refs, wout_refs):
            if len(a_ref.shape) == 2:
                a = a_ref[:, rows]
            else:
                a = jnp.concatenate([a_ref[hh, s] for hh in range(a_ref.shape[0])], axis=0)
            a = a.astype(F32).T.astype(BF16)
            x = x + jnp.dot(a, w_ref[...], preferred_element_type=F32)
        o_ref[rows, :] = x
        h_ref[rows, :] = _rms_rows(x, g_ref[...]).astype(BF16)
    scale = hd ** -0.5 * LOG2E
    for cols in subs:
        qt = lax.dot_general(wq_ref[...], h_ref[cols, :], _NT, preferred_element_type=F32)
        for hh in range(CROSS_HEADS):
            blk = qt[hh * hd:(hh + 1) * hd, :]
            qn_ref[hh, :, cols] = (blk * (_rms_scale_cols(blk) * scale)).astype(BF16)
    denoms = {}
    for s, cols in enumerate(subs):
        for hh in range(CROSS_HEADS):
            sc = jnp.dot(k_ref[0, 0, hh], qn_ref[hh, :, cols], preferred_element_type=F32)
            p = jnp.exp2(sc - jnp.max(sc, axis=0, keepdims=True))
            denoms[s, hh] = jnp.sum(p, axis=0, keepdims=True)
            p_ref[hh, :, cols] = p.astype(BF16)
    for s, cols in enumerate(subs):
        for hh in range(CROSS_HEADS):
            o = jnp.dot(v_ref[0, 0, hh], p_ref[hh, :, cols], preferred_element_type=F32)
            ot_ref[hh * hd:(hh + 1) * hd, cols] = o / denoms[s, hh]
    for rows in subs:
        o_ref[rows, :] += jnp.dot(ot_ref[:, rows].T.astype(BF16), wo_ref[...],
                                  preferred_element_type=F32)


def _post_mixer(x, attn_list, wout_list, gain, wq_t, mem_k, mem_v, wo, layer, seq):
    T, D = x.shape
    tile = POST_TILE
    per_seq = seq // tile
    n_attn = len(attn_list)
    const = lambda i: (0, 0)
    in_specs = [pl.BlockSpec((tile, D), lambda i: (i, 0))]
    for a in attn_list:
        if a.ndim == 2:
            in_specs.append(pl.BlockSpec((a.shape[0], tile), lambda i: (0, i)))
        else:
            in_specs.append(pl.BlockSpec((a.shape[0], tile // TB) + a.shape[2:],
                                         lambda i: (0, i, 0, 0)))
    in_specs += [pl.BlockSpec(w.shape, const) for w in wout_list]
    in_specs += [pl.BlockSpec((1, D), const), pl.BlockSpec(wq_t.shape, const),
                 pl.BlockSpec((1, 1) + mem_k.shape[2:], lambda i: (layer, i // per_seq, 0, 0, 0)),
                 pl.BlockSpec((1, 1) + mem_v.shape[2:], lambda i: (layer, i // per_seq, 0, 0, 0)),
                 pl.BlockSpec(wo.shape, const)]
    return pl.pallas_call(
        functools.partial(_post_kernel, n_attn=n_attn),
        grid=(T // tile,),
        in_specs=in_specs,
        out_specs=pl.BlockSpec((tile, D), lambda i: (i, 0)),
        out_shape=jax.ShapeDtypeStruct((T, D), F32),
        scratch_shapes=[pltpu.VMEM((tile, D), BF16),
                        pltpu.VMEM((CROSS_HEADS, CROSS_HEAD_DIM, tile), BF16),
                        pltpu.VMEM((CROSS_HEADS, mem_k.shape[3], tile), BF16),
                        pltpu.VMEM((CROSS_HEADS * CROSS_HEAD_DIM, tile), F32)],
        compiler_params=_params(1),
        name="outproj_cross",
    )(x, *attn_list, *wout_list, gain, wq_t, mem_k, mem_v, wo)


def _ffn_kernel(x_ref, g_ref, w1_ref, cw_ref, w2_ref, o_ref, tail_ref, ext_ref, h_ref, act_ref, *,
                per_seq):
    i = pl.program_id(0)
    tile = x_ref.shape[0]
    n_chunks = w2_ref.shape[0]
    lead = 8
    x = x_ref[...]
    h_ref[...] = _rms_rows(x, g_ref[...]).astype(BF16)

    @pl.when(i % per_seq == 0)
    def _():
        tail_ref[...] = jnp.zeros(tail_ref.shape, F32)

    o_ref[...] = x

    def project(c, slot):
        for part in range(2):
            cc = c + part * n_chunks
            u = jnp.dot(h_ref[...], w1_ref[cc], preferred_element_type=F32)
            ext_ref[slot, part, 0:lead, :] = tail_ref[cc]
            ext_ref[slot, part, lead:, :] = u
            tail_ref[cc] = u[tile - lead:, :]

    def gate_up(c, slot):
        halves = []
        for part in range(2):
            cw = cw_ref[c + part * n_chunks]
            ext = ext_ref.at[slot, part]
            halves.append(cw[3:4, :] + cw[0:1, :] * ext[lead - 2:lead - 2 + tile, :]
                          + cw[1:2, :] * ext[lead - 1:lead - 1 + tile, :]
                          + cw[2:3, :] * ext[lead:, :])
        gate, up = halves
        act_ref[c] = (gate * (1.0 / (1.0 + jnp.exp(-gate))) * up).astype(BF16)

    def down(group):
        a = jnp.concatenate([act_ref[c] for c in group], axis=1)
        w = jnp.concatenate([w2_ref[c] for c in group], axis=0)
        o_ref[...] += jnp.dot(a, w, preferred_element_type=F32)

    groups = [list(range(s, min(s + FFN_GROUP, n_chunks))) for s in range(0, n_chunks, FFN_GROUP)]

    def slot(gi, k):
        return (gi % 2) * FFN_GROUP + k

    for k, c in enumerate(groups[0]):
        project(c, slot(0, k))
    for gi, group in enumerate(groups):
        if gi + 1 < len(groups):
            for k, c in enumerate(groups[gi + 1]):
                project(c, slot(gi + 1, k))
        for k, c in enumerate(group):
            gate_up(c, slot(gi, k))
        down(group)


def _conv_ffn(x, gain, w1, cw, w2, seq):
    T, D = x.shape
    tile = FFN_TILE
    const3 = lambda i: (0, 0, 0)
    return pl.pallas_call(
        functools.partial(_ffn_kernel, per_seq=seq // tile),
        grid=(T // tile,),
        in_specs=[pl.BlockSpec((tile, D), lambda i: (i, 0)),
                  pl.BlockSpec((1, D), lambda i: (0, 0)),
                  pl.BlockSpec(w1.shape, const3, pipeline_mode=pl.Buffered(1)),
                  pl.BlockSpec(cw.shape, const3),
                  pl.BlockSpec(w2.shape, const3, pipeline_mode=pl.Buffered(1))],
        out_specs=pl.BlockSpec((tile, D), lambda i: (i, 0)),
        out_shape=jax.ShapeDtypeStruct((T, D), F32),
        scratch_shapes=[pltpu.VMEM((w1.shape[0], 8, FFN_CHUNK), F32),
                        pltpu.VMEM((min(2 * FFN_GROUP, -(-w2.shape[0] // FFN_GROUP) * FFN_GROUP),
                                    2, tile + 8, FFN_CHUNK), F32),
                        pltpu.VMEM((tile, D), BF16),
                        pltpu.VMEM((w2.shape[0], tile, FFN_CHUNK), BF16)],
        compiler_params=_params(1),
        name="conv_ffn",
    )(x, gain, w1, cw, w2)


def _t5_bucket(dist):
    n = jnp.maximum(dist, 0)
    max_exact = NUM_BUCKETS // 2
    nf = jnp.maximum(n, 1).astype(F32)
    coef = (NUM_BUCKETS - max_exact) / math.log(MAX_DISTANCE / max_exact)
    large = max_exact + (jnp.log(nf / max_exact) * coef).astype(jnp.int32)
    large = jnp.minimum(large, NUM_BUCKETS - 1)
    return jnp.where(n < max_exact, n, large)


def _toeplitz_kernel(w_ref, o_ref, *, window):
    n_tiles, rows, cols = o_ref.shape[1:]
    period = w_ref.shape[-1]
    key = lax.broadcasted_iota(jnp.int32, (rows, cols), 0)
    qry = lax.broadcasted_iota(jnp.int32, (rows, cols), 1)
    for d in range(n_tiles):
        spread = jnp.broadcast_to(w_ref[0, d], (rows, period))
        skew = pltpu.roll(spread, 0, 1, stride=1, stride_axis=0)
        tile = skew[:, rows:rows + cols]
        if window is not None:
            tile = jnp.where(jnp.logical_and(qry - key + window >= 0, qry - key < 0), tile, NEG_INF)
        elif d == 0:
            tile = jnp.where(qry >= key, tile, NEG_INF)
        o_ref[0, d] = tile


def _toeplitz_tiles(windows, rows, cols, window):
    n_heads, n_tiles, _, period = windows.shape
    assert period >= rows + cols and rows % 128 == 0 and cols % 128 == 0 and period % 128 == 0
    return pl.pallas_call(
        functools.partial(_toeplitz_kernel, window=window),
        grid=(n_heads,),
        in_specs=[pl.BlockSpec((1, n_tiles, 1, period), lambda h: (h, 0, 0, 0))],
        out_specs=pl.BlockSpec((1, n_tiles, rows, cols), lambda h: (h, 0, 0, 0)),
        out_shape=jax.ShapeDtypeStruct((n_heads, n_tiles, rows, cols), F32),
        compiler_params=_params(1),
        name="bias_tiles",
    )(windows)


def _moba_bias_tiles(bias_by_dist):
    n_tiles = min(bias_by_dist.shape[1] // TB, -(-(MAX_DISTANCE + TB) // TB))
    period = 3 * TB
    padded = jnp.pad(bias_by_dist, ((0, 0), (TB, period)))
    windows = jnp.stack([padded[:, d * TB:d * TB + period] for d in range(n_tiles)], axis=1)
    return _toeplitz_tiles(windows[:, :, None, :], TB, TB, None)


def _swa_bias_tiles(bias_by_dist):
    w = SWA_WINDOW
    windows = jnp.pad(bias_by_dist[:, :3 * w], ((0, 0), (w, 0)))
    return _toeplitz_tiles(windows[:, None, None, :], 2 * w, w, w)[:, 0]


def kernel(x, mem, rel_bias, mix_norm, ev_w_in, ev_w_out, ev_q_gain, ev_k_gain, od_w_in, od_w_out, od_q_gain, od_k_gain, od_sinks, cx_norm, cx_mem_norm, cx_w_q, cx_w_kv, cx_w_o, cx_q_gain, cx_k_gain, ff_norm, ff_w_in, ff_conv_w, ff_conv_b, ff_w_out):
    B, S, D = x.shape
    T = B * S
    depth = mix_norm.shape[0]
    n_heads = rel_bias.shape[1]
    sb_heads = n_heads // 2
    d_ff = ff_w_out.shape[1]
    n_chunks = d_ff // FFN_CHUNK
    n_kv = (od_w_in.shape[2] // HEAD_DIM - n_heads) // 2
    assert S % FFN_TILE == 0 and S % POST_TILE == 0 and S % TB == 0 and d_ff % FFN_CHUNK == 0

    def lane_bcast(col, width):
        return jnp.broadcast_to(col[:, None], (col.shape[0], width)).astype(F32)

    bias_by_dist = rel_bias.astype(F32)[_t5_bucket(jnp.arange(S + TB, dtype=jnp.int32))].T
    moba_tiles = _moba_bias_tiles(bias_by_dist[sb_heads:, :S] * LOG2E)
    swa_tiles = _swa_bias_tiles(bias_by_dist * LOG2E)

    cx_kgain = jnp.stack([lane_bcast(cx_k_gain[l] * cx_q_gain[l], mem.shape[1]) for l in range(depth)])
    mem_k, mem_v = _mem_kv(mem, cx_mem_norm[:, None, :], jnp.swapaxes(cx_w_kv, 1, 2).astype(BF16),
                           cx_kgain)

    xf = x.reshape(T, D)
    for layer in range(depth):
        idx = layer // 2
        gain = mix_norm[layer][None, :]
        if layer % 2 == 0:
            kgain = lane_bcast(jnp.tile(ev_k_gain[idx] * ev_q_gain[idx], sb_heads), TB)
            qa, ka, va, qb, kb, vb = _even_inproj(xf, gain, ev_w_in[idx].T.astype(BF16), kgain)
            attn = [_sb_attention(qa, ka, va, B), _moba_attention(qb, kb, vb, moba_tiles, B)]
            half = sb_heads * HEAD_DIM
            wout = [ev_w_out[idx, :half].astype(BF16), ev_w_out[idx, half:].astype(BF16)]
        else:
            kgain = lane_bcast(jnp.tile(od_k_gain[idx] * od_q_gain[idx], n_kv), TB)
            q, k, v = _odd_inproj(xf, gain, od_w_in[idx].T.astype(BF16), kgain, n_heads, n_kv)
            attn = [_swa_attention(q, k, v, swa_tiles, od_sinks[idx].astype(F32), B)]
            wout = [od_w_out[idx].astype(BF16)]
        xf = _post_mixer(xf, attn, wout, cx_norm[layer][None, :], cx_w_q[layer].T.astype(BF16),
                         mem_k, mem_v, cx_w_o[layer].astype(BF16), layer, S)
        w1 = ff_w_in[layer].reshape(D, 2 * n_chunks, FFN_CHUNK).transpose(1, 0, 2).astype(BF16)
        cw = jnp.concatenate([ff_conv_w[layer], ff_conv_b[layer][None, :]], axis=0)
        cw = cw.reshape(CONV_WIDTH + 1, 2 * n_chunks, FFN_CHUNK).transpose(1, 0, 2)
        w2 = ff_w_out[layer].reshape(n_chunks, FFN_CHUNK, D).astype(BF16)
        xf = _conv_ffn(xf, ff_norm[layer][None, :], w1, cw, w2, S)
    return xf.reshape(B, S, D)
```

```python
import functools
import math

import jax
import jax.numpy as jnp
import numpy as np
from jax import lax
from jax.experimental import pallas as pl
from jax.experimental.pallas import tpu as pltpu

F32 = jnp.float32
BF16 = jnp.bfloat16

HEAD_DIM = 64
PAIR = 2 * HEAD_DIM
TB = 256
SWA_WINDOW = 128
MOBA_TOPK = 3
MOBA_UNROLL = 4
MOBA_HEADS_PER_STEP = 2
MOBA_PAIRS_PER_TRIP = 4
LOG2E = math.log2(math.e)
NUM_BUCKETS = 32
MAX_DISTANCE = 4096
CROSS_HEADS = 4
CROSS_HEAD_DIM = 128
CONV_WIDTH = 3
RMS_EPS = 1e-6
NEG_INF = -1e30
SB_DEAD_LOG2 = -150.5
SB_HEADS_PER_STEP = 8
FFN_CHUNK = 256
FFN_TILE = 512
FFN_GROUP = 6
POST_TILE = 512
VMEM_LIMIT = 56 * 1024 * 1024

_NT = (((1,), (1,)), ((), ()))


def _rms_rows(x, gain):
    ms = jnp.mean(x * x, axis=-1, keepdims=True)
    return (x * lax.rsqrt(ms + RMS_EPS)) * gain


def _rms_scale_cols(xt):
    return lax.rsqrt(jnp.mean(xt * xt, axis=0, keepdims=True) + RMS_EPS)


def _params(n_axes):
    return pltpu.CompilerParams(dimension_semantics=("arbitrary",) * n_axes,
                                vmem_limit_bytes=VMEM_LIMIT)


def _pair_extend(q, head):
    odd = head % 2 == 1
    zero = jnp.zeros_like(q)
    return jnp.concatenate([jnp.where(odd, zero, q), jnp.where(odd, q, zero)], axis=0)


def _even_inproj_kernel(x_ref, g_ref, wt_ref, kg_ref,
                        qa_ref, ka_ref, va_ref, qb_ref, kb_ref, vb_ref):
    h = _rms_rows(x_ref[...], g_ref[...]).astype(BF16)
    width = 8 * HEAD_DIM

    def proj(group):
        w = wt_ref[group * width:(group + 1) * width, :]
        return lax.dot_general(w, h, _NT, preferred_element_type=F32)

    def heads(t):
        return [t[hh * HEAD_DIM:(hh + 1) * HEAD_DIM, :] for hh in range(8)]

    scale = HEAD_DIM ** -0.5
    for hh, blk in enumerate(heads(proj(0))):
        qa_ref[hh, 0] = (blk * (scale * LOG2E)).astype(BF16)
    ka_ref[...] = proj(1).T.astype(BF16)
    for hh, blk in enumerate(heads(proj(2))):
        va_ref[hh, 0] = blk.astype(BF16)
    for hh, blk in enumerate(heads(proj(3))):
        qb_ref[hh, 0] = (blk * (_rms_scale_cols(blk) * (scale * LOG2E))).astype(BF16)
    one_then_zeros = jnp.where(lax.broadcasted_iota(jnp.int32, (HEAD_DIM, TB), 0) == 0, 1.0, 0.0)
    kg = kg_ref[...]
    kb = []
    for hh, blk in enumerate(heads(proj(4))):
        kb += [blk * _rms_scale_cols(blk) * kg[hh * HEAD_DIM:(hh + 1) * HEAD_DIM, :], one_then_zeros]
    kb_ref[...] = jnp.concatenate(kb, axis=0).T.astype(BF16)
    for hh, blk in enumerate(heads(proj(5))):
        vb_ref[hh, 0] = blk.astype(BF16)


def _even_inproj(x, gain, wt, kgain):
    T, D = x.shape
    nblk = T // TB
    blocked = jax.ShapeDtypeStruct((8, nblk, HEAD_DIM, TB), BF16)
    rowmajor = jax.ShapeDtypeStruct((T, 8 * HEAD_DIM), BF16)
    bspec = pl.BlockSpec((8, 1, HEAD_DIM, TB), lambda i: (0, i, 0, 0))
    rspec = pl.BlockSpec((TB, 8 * HEAD_DIM), lambda i: (i, 0))
    return pl.pallas_call(
        _even_inproj_kernel,
        grid=(nblk,),
        in_specs=[pl.BlockSpec((TB, D), lambda i: (i, 0)),
                  pl.BlockSpec((1, D), lambda i: (0, 0)),
                  pl.BlockSpec(wt.shape, lambda i: (0, 0)),
                  pl.BlockSpec(kgain.shape, lambda i: (0, 0))],
        out_specs=[bspec, rspec, bspec, bspec, pl.BlockSpec((TB, 8 * PAIR), lambda i: (i, 0)), bspec],
        out_shape=[blocked, rowmajor, blocked, blocked,
                   jax.ShapeDtypeStruct((T, 8 * PAIR), BF16), blocked],
        compiler_params=_params(1),
        name="even_inproj",
    )(x, gain, wt, kgain)


def _odd_inproj_kernel(x_ref, g_ref, wt_ref, kg_ref, q_ref, k_ref, v_ref, *, n_q, n_kv):
    h = _rms_rows(x_ref[...], g_ref[...]).astype(BF16)
    scale = HEAD_DIM ** -0.5 * LOG2E
    group = 8 * HEAD_DIM
    for g in range(n_q * HEAD_DIM // group):
        w = wt_ref[g * group:(g + 1) * group, :]
        t = lax.dot_general(w, h, _NT, preferred_element_type=F32)
        for hh in range(8):
            blk = t[hh * HEAD_DIM:(hh + 1) * HEAD_DIM, :]
            q_ref[g * 8 + hh, 0] = (blk * (_rms_scale_cols(blk) * scale)).astype(BF16)
    kv_rows = n_kv * HEAD_DIM
    w = wt_ref[n_q * HEAD_DIM:n_q * HEAD_DIM + 2 * kv_rows, :]
    t = lax.dot_general(w, h, _NT, preferred_element_type=F32)
    kparts = []
    for hh in range(n_kv):
        blk = t[hh * HEAD_DIM:(hh + 1) * HEAD_DIM, :]
        kparts.append(blk * _rms_scale_cols(blk))
        v_ref[hh, 0] = t[kv_rows + hh * HEAD_DIM:kv_rows + (hh + 1) * HEAD_DIM, :].astype(BF16)
    k_ref[...] = (jnp.concatenate(kparts, axis=0) * kg_ref[...]).T.astype(BF16)


def _odd_inproj(x, gain, wt, kgain, n_q, n_kv):
    T, D = x.shape
    nblk = T // TB
    return pl.pallas_call(
        functools.partial(_odd_inproj_kernel, n_q=n_q, n_kv=n_kv),
        grid=(nblk,),
        in_specs=[pl.BlockSpec((TB, D), lambda i: (i, 0)),
                  pl.BlockSpec((1, D), lambda i: (0, 0)),
                  pl.BlockSpec(wt.shape, lambda i: (0, 0)),
                  pl.BlockSpec(kgain.shape, lambda i: (0, 0))],
        out_specs=[pl.BlockSpec((n_q, 1, HEAD_DIM, TB), lambda i: (0, i, 0, 0)),
                   pl.BlockSpec((TB, n_kv * HEAD_DIM), lambda i: (i, 0)),
                   pl.BlockSpec((n_kv, 1, HEAD_DIM, TB), lambda i: (0, i, 0, 0))],
        out_shape=[jax.ShapeDtypeStruct((n_q, nblk, HEAD_DIM, TB), BF16),
                   jax.ShapeDtypeStruct((T, n_kv * HEAD_DIM), BF16),
                   jax.ShapeDtypeStruct((n_kv, nblk, HEAD_DIM, TB), BF16)],
        compiler_params=_params(1),
        name="odd_inproj",
    )(x, gain, wt, kgain)


def _sb_kernel(q_ref, k_ref, v_ref, o_ref, carry_ref, acc_ref, lb_ref, hl_ref, w_ref):
    i = pl.program_id(2)
    n_heads = q_ref.shape[0]
    key_idx = lax.broadcasted_iota(jnp.int32, (TB, TB), 0)
    qry_idx = lax.broadcasted_iota(jnp.int32, (TB, TB), 1)
    later = jnp.where(qry_idx > key_idx, 1.0, 0.0).astype(BF16)
    later2 = jnp.concatenate([later, later], axis=1)
    past = key_idx < qry_idx
    zeros = jnp.zeros((HEAD_DIM, TB), BF16)

    def group(j, diagonal):
        rows = pl.ds(pl.multiple_of(j * TB, TB), TB)
        colsums = []
        for hh in range(n_heads):
            q = q_ref[hh, 0]
            q_ext = jnp.concatenate([q, zeros] if hh % 2 == 0 else [zeros, q], axis=0)
            kj = k_ref[rows, (hh // 2) * PAIR:(hh // 2 + 1) * PAIR]
            z = jnp.dot(kj, q_ext, preferred_element_type=F32)
            log_beta = jnp.minimum(z, 0.0) - jnp.log2(1.0 + jnp.exp2(-jnp.abs(z)))
            log_not = log_beta - z
            if diagonal:
                log_not = jnp.where(past, log_not, 0.0)
            hi = log_not.astype(BF16)
            lb_ref[hh] = log_beta
            hl_ref[hh, :TB, :] = hi
            hl_ref[hh, TB:, :] = (log_not - hi.astype(F32)).astype(BF16)
            colsums.append(jnp.sum(log_not, axis=0, keepdims=True))
        carries = []
        for hh in range(n_heads):
            between = jnp.dot(later2, hl_ref[hh], preferred_element_type=F32)
            carry = 0.0 if diagonal else carry_ref[hh]
            w = jnp.exp2(lb_ref[hh] + between + carry)
            if diagonal:
                w = jnp.where(past, w, 0.0)
            w_ref[hh] = w.astype(BF16)
            carries.append(carry + colsums[hh])
            carry_ref[hh] = carries[hh]
        for hh in range(n_heads):
            pv = jnp.dot(v_ref[hh, j], w_ref[hh], preferred_element_type=F32)
            acc_ref[hh] = pv if diagonal else acc_ref[hh] + pv
        return carries

    def alive_flag(carries):
        top = functools.reduce(jnp.maximum, carries)
        return (jnp.max(top) > SB_DEAD_LOG2).astype(jnp.int32)

    alive = alive_flag(group(i, True))

    def cond(state):
        return jnp.logical_and(state[0] >= 0, state[1] > 0)

    def body(state):
        j = state[0]
        return j - 1, alive_flag(group(j, False))

    lax.while_loop(cond, body, (i - 1, alive))
    for hh in range(n_heads):
        o_ref[hh * HEAD_DIM:(hh + 1) * HEAD_DIM, :] = acc_ref[hh].astype(BF16)


def _sb_attention(qt, k, vt, batch):
    n_heads, nblk = qt.shape[0], qt.shape[1]
    nq = nblk // batch
    S = nq * TB
    step = SB_HEADS_PER_STEP
    assert n_heads % step == 0 and step % 2 == 0
    return pl.pallas_call(
        _sb_kernel,
        grid=(batch, n_heads // step, nq),
        in_specs=[pl.BlockSpec((step, 1, HEAD_DIM, TB), lambda b, h, i: (h, b * nq + i, 0, 0)),
                  pl.BlockSpec((S, step * HEAD_DIM), lambda b, h, i: (b, h)),
                  pl.BlockSpec((step, nq, HEAD_DIM, TB), lambda b, h, i: (h, b, 0, 0))],
        out_specs=pl.BlockSpec((step * HEAD_DIM, TB), lambda b, h, i: (h, b * nq + i)),
        out_shape=jax.ShapeDtypeStruct((n_heads * HEAD_DIM, nblk * TB), BF16),
        scratch_shapes=[pltpu.VMEM((step, 1, TB), F32), pltpu.VMEM((step, HEAD_DIM, TB), F32),
                        pltpu.VMEM((step, TB, TB), F32), pltpu.VMEM((step, 2 * TB, TB), BF16),
                        pltpu.VMEM((step, TB, TB), BF16)],
        compiler_params=_params(3),
        name="stickbreak_attn",
    )(qt, k, vt)


def _moba_kernel(q_ref, k_ref, v_ref, bias_ref, o_ref, kmean_ref, mask_ref, s_ref, p_ref, *,
                 n_tiles):
    n_heads, nq = q_ref.shape[0], q_ref.shape[1]
    heads = range(n_heads)

    def key_block(hh, j):
        return k_ref[pl.ds(pl.multiple_of(j * TB, TB), TB), hh * PAIR:(hh + 1) * PAIR]

    def block_means(jb, _):
        for hh in heads:
            blk = key_block(hh, jb).astype(F32)
            kmean_ref[hh, pl.ds(jb, 1), :] = jnp.mean(blk, axis=0, keepdims=True)
        return 0

    lax.fori_loop(0, nq, block_means, 0)

    blk_idx = lax.broadcasted_iota(jnp.int32, (nq, TB), 0)

    def choose(i, _):
        for hh in heads:
            q_gate = jnp.concatenate([q_ref[hh, i], jnp.zeros((PAIR - HEAD_DIM, TB), BF16)],
                                     axis=0)
            km = kmean_ref[hh]
            km_hi = km.astype(BF16)
            km_lo = (km - km_hi.astype(F32)).astype(BF16)
            gate = (jnp.dot(km_hi, q_gate, preferred_element_type=F32)
                    + jnp.dot(km_lo, q_gate, preferred_element_type=F32))
            gate = jnp.where(blk_idx < i, gate, NEG_INF)
            chosen = blk_idx == i
            for _ in range(MOBA_TOPK):
                best = jnp.max(gate, axis=0, keepdims=True)
                cand = jnp.where(jnp.logical_and(gate == best, best > 0.5 * NEG_INF),
                                 blk_idx, nq)
                pick = blk_idx == jnp.min(cand, axis=0, keepdims=True)
                chosen = jnp.logical_or(chosen, pick)
                gate = jnp.where(pick, NEG_INF, gate)
            mask_ref[hh, i] = jnp.where(chosen, 0.0, NEG_INF)
        return 0

    def choose_some(t, _):
        for k in range(MOBA_UNROLL):
            choose(t * MOBA_UNROLL + k, 0)
        return 0

    lax.fori_loop(0, nq // MOBA_UNROLL, choose_some, 0)

    first_row = lax.broadcasted_iota(jnp.int32, (16, TB), 0) == 0
    ones_row = jnp.where(first_row, 1.0, 0.0).astype(BF16)
    q_tail = jnp.zeros((PAIR - HEAD_DIM - 16, TB), BF16)

    def following(item):
        i, g = item
        wrap = g == i // MOBA_UNROLL
        stay = jnp.logical_and(wrap, i == nq - 1)
        return (jnp.where(jnp.logical_and(wrap, jnp.logical_not(stay)), i + 1, i),
                jnp.where(stay, g, jnp.where(wrap, 0, g + 1)))

    def score_group(hh, item, slot):
        i, g = item
        q = q_ref[hh, i]
        raw = []
        for u in range(MOBA_UNROLL):
            j = g * MOBA_UNROLL + u
            mask_rows = jnp.where(first_row, mask_ref[hh, i, pl.ds(j, 1), :], 0.0).astype(BF16)
            q_ext = jnp.concatenate([q, mask_rows, q_tail], axis=0)
            raw.append(jnp.dot(key_block(hh, j), q_ext, preferred_element_type=F32))
        top = jnp.full((1, TB), NEG_INF, F32)
        for u in range(MOBA_UNROLL):
            tile = jnp.clip(i - (g * MOBA_UNROLL + u), 0, n_tiles - 1)
            s = raw[u] + bias_ref[hh, tile]
            s_ref[hh, slot, u] = s
            top = jnp.maximum(top, jnp.max(s, axis=0, keepdims=True))
        return top

    def value_group(hh, item, slot, top, state):
        i, g = item
        fresh = g == 0
        m = jnp.where(fresh, NEG_INF, state[0])
        acc = jnp.where(fresh, 0.0, state[1])
        m_new = jnp.maximum(m, top)
        for u in range(MOBA_UNROLL):
            p_ref[hh, slot, u] = jnp.exp2(s_ref[hh, slot, u] - m_new).astype(BF16)
        pv = None
        for u in range(MOBA_UNROLL):
            v_ext = jnp.concatenate([v_ref[hh, g * MOBA_UNROLL + u], ones_row], axis=0)
            part = jnp.dot(v_ext, p_ref[hh, slot, u], preferred_element_type=F32)
            pv = part if pv is None else pv + part
        acc = jnp.exp2(m - m_new) * acc + pv
        o_ref[hh, i] = (acc[:HEAD_DIM] / acc[HEAD_DIM:HEAD_DIM + 1]).astype(BF16)
        return m_new, acc

    def pair(carry):
        item_a, tops_a, states = carry
        item_b = following(item_a)
        item_c = following(item_b)
        tops_b = [score_group(hh, item_b, 1) for hh in heads]
        states = [value_group(hh, item_a, 0, tops_a[hh], states[hh]) for hh in heads]
        tops_c = [score_group(hh, item_c, 0) for hh in heads]
        states = [value_group(hh, item_b, 1, tops_b[hh], states[hh]) for hh in heads]
        return item_c, tops_c, states

    n_groups = nq // MOBA_UNROLL
    n_items = MOBA_UNROLL * n_groups * (n_groups + 1) // 2
    pairs_per_trip = max(p for p in range(1, MOBA_PAIRS_PER_TRIP + 1) if n_items % (2 * p) == 0)

    def trip(_, carry):
        for _ in range(pairs_per_trip):
            carry = pair(carry)
        return carry

    first = (jnp.int32(0), jnp.int32(0))
    init = [(jnp.full((1, TB), NEG_INF, F32), jnp.zeros((HEAD_DIM + 16, TB), F32))
            for _ in heads]
    lax.fori_loop(0, n_items // (2 * pairs_per_trip), trip,
                  (first, [score_group(hh, first, 0) for hh in heads], init))


def _moba_attention(qt, k, vt, bias_tiles, batch):
    n_heads, nblk = qt.shape[0], qt.shape[1]
    nq = nblk // batch
    S = nq * TB
    n_tiles = bias_tiles.shape[1]
    step = MOBA_HEADS_PER_STEP
    assert n_heads % step == 0 and nq % MOBA_UNROLL == 0
    return pl.pallas_call(
        functools.partial(_moba_kernel, n_tiles=n_tiles),
        grid=(n_heads // step, batch),
        in_specs=[pl.BlockSpec((step, nq, HEAD_DIM, TB), lambda h, b: (h, b, 0, 0)),
                  pl.BlockSpec((S, step * PAIR), lambda h, b: (b, h)),
                  pl.BlockSpec((step, nq, HEAD_DIM, TB), lambda h, b: (h, b, 0, 0)),
                  pl.BlockSpec((step, n_tiles, TB, TB), lambda h, b: (h, 0, 0, 0),
                               pipeline_mode=pl.Buffered(1))],
        out_specs=pl.BlockSpec((step, nq, HEAD_DIM, TB), lambda h, b: (h, b, 0, 0)),
        out_shape=jax.ShapeDtypeStruct((n_heads, nblk, HEAD_DIM, TB), BF16),
        scratch_shapes=[pltpu.VMEM((step, nq, PAIR), F32), pltpu.VMEM((step, nq, nq, TB), F32),
                        pltpu.VMEM((step, 2, MOBA_UNROLL, TB, TB), F32),
                        pltpu.VMEM((step, 2, MOBA_UNROLL, TB, TB), BF16)],
        compiler_params=_params(2),
        name="moba_attn",
    )(qt, k, vt, bias_tiles)


def _swa_kernel(sink_ref, q_ref, kp_ref, kc_ref, vp_ref, vc_ref, bias_ref, o_ref, s_ref, p_ref):
    kv_head = pl.program_id(0)
    i = pl.program_id(2)
    group = q_ref.shape[0]
    w = SWA_WINDOW
    kp, kc = kp_ref[...], kc_ref[...]
    vp, vc = vp_ref[0, 0], vc_ref[0, 0]
    keys = [jnp.concatenate([kp[TB - w:, :], kc[:w, :]], axis=0), kc]
    vals = [jnp.concatenate([vp[:, TB - w:], vc[:, :w]], axis=1), vc]
    first_rows_exist = jnp.logical_or(
        lax.broadcasted_iota(jnp.int32, (2 * w, w), 0) >= w, i > 0)
    n_halves = TB // w
    sinks = [sink_ref[kv_head * group + hh] * LOG2E for hh in range(group)]
    maxes, denoms = [], []
    for hh in range(group):
        q_ext = _pair_extend(q_ref[hh, 0], kv_head)
        for u in range(n_halves):
            s = (jnp.dot(keys[u], q_ext[:, u * w:(u + 1) * w], preferred_element_type=F32)
                 + bias_ref[hh])
            if u == 0:
                s = jnp.where(first_rows_exist, s, NEG_INF)
            s_ref[hh * n_halves + u] = s
            maxes.append(jnp.maximum(jnp.max(s, axis=0, keepdims=True), sinks[hh]))
    for c in range(group * n_halves):
        p = jnp.exp2(s_ref[c] - maxes[c])
        denoms.append(jnp.sum(p, axis=0, keepdims=True) + jnp.exp2(sinks[c // n_halves] - maxes[c]))
        p_ref[c] = p.astype(BF16)
    for hh in range(group):
        halves = [jnp.dot(vals[u], p_ref[hh * n_halves + u], preferred_element_type=F32)
                  / denoms[hh * n_halves + u] for u in range(n_halves)]
        o_ref[hh * HEAD_DIM:(hh + 1) * HEAD_DIM, :] = jnp.concatenate(halves, axis=1).astype(BF16)


def _swa_attention(qt, k, vt, bias_tiles, sinks, batch):
    n_heads, nblk = qt.shape[0], qt.shape[1]
    n_kv = vt.shape[0]
    group = n_heads // n_kv
    nq = nblk // batch
    assert TB == 2 * SWA_WINDOW

    def prev(b, i):
        return b * nq + jnp.maximum(i - 1, 0)

    return pl.pallas_call(
        _swa_kernel,
        grid_spec=pltpu.PrefetchScalarGridSpec(
            num_scalar_prefetch=1,
            grid=(n_kv, batch, nq),
            in_specs=[
                pl.BlockSpec((group, 1, HEAD_DIM, TB), lambda g, b, i, s: (g, b * nq + i, 0, 0)),
                pl.BlockSpec((TB, PAIR), lambda g, b, i, s: (prev(b, i), g // 2)),
                pl.BlockSpec((TB, PAIR), lambda g, b, i, s: (b * nq + i, g // 2)),
                pl.BlockSpec((1, 1, HEAD_DIM, TB), lambda g, b, i, s: (g, prev(b, i), 0, 0)),
                pl.BlockSpec((1, 1, HEAD_DIM, TB), lambda g, b, i, s: (g, b * nq + i, 0, 0)),
                pl.BlockSpec((group, 2 * SWA_WINDOW, SWA_WINDOW), lambda g, b, i, s: (g, 0, 0)),
            ],
            out_specs=pl.BlockSpec((group * HEAD_DIM, TB), lambda g, b, i, s: (g, b * nq + i)),
            scratch_shapes=[pltpu.VMEM((2 * group, 2 * SWA_WINDOW, SWA_WINDOW), F32),
                            pltpu.VMEM((2 * group, 2 * SWA_WINDOW, SWA_WINDOW), BF16)],
        ),
        out_shape=jax.ShapeDtypeStruct((n_heads * HEAD_DIM, nblk * TB), BF16),
        compiler_params=_params(3),
        name="swa_attn",
    )(sinks, qt, k, k, vt, vt, bias_tiles)


def _mem_kv_kernel(mem_ref, g_ref, wt_ref, kg_ref, k_ref, v_ref):
    h = _rms_rows(mem_ref[0], g_ref[0]).astype(BF16)
    t = lax.dot_general(wt_ref[0], h, _NT, preferred_element_type=F32)
    cw = CROSS_HEADS * CROSS_HEAD_DIM
    for hh in range(CROSS_HEADS):
        blk = t[hh * CROSS_HEAD_DIM:(hh + 1) * CROSS_HEAD_DIM, :]
        kn = blk * _rms_scale_cols(blk) * kg_ref[0]
        k_ref[0, 0, hh] = kn.T.astype(BF16)
        v_ref[0, 0, hh] = t[cw + hh * CROSS_HEAD_DIM:cw + (hh + 1) * CROSS_HEAD_DIM, :].astype(BF16)


def _mem_kv(mem, gains, wts, kgains):
    B, M, D = mem.shape
    depth = gains.shape[0]
    kshape = (depth, B, CROSS_HEADS, M, CROSS_HEAD_DIM)
    vshape = (depth, B, CROSS_HEADS, CROSS_HEAD_DIM, M)
    return pl.pallas_call(
        _mem_kv_kernel,
        grid=(depth, B),
        in_specs=[pl.BlockSpec((1, M, D), lambda l, b: (b, 0, 0)),
                  pl.BlockSpec((1, 1, D), lambda l, b: (l, 0, 0)),
                  pl.BlockSpec((1,) + wts.shape[1:], lambda l, b: (l, 0, 0)),
                  pl.BlockSpec((1,) + kgains.shape[1:], lambda l, b: (l, 0, 0))],
        out_specs=[pl.BlockSpec((1, 1) + kshape[2:], lambda l, b: (l, b, 0, 0, 0)),
                   pl.BlockSpec((1, 1) + vshape[2:], lambda l, b: (l, b, 0, 0, 0))],
        out_shape=[jax.ShapeDtypeStruct(kshape, BF16), jax.ShapeDtypeStruct(vshape, BF16)],
        compiler_params=_params(2),
        name="mem_kv",
    )(mem, gains, wts, kgains)


def _post_kernel(*refs, n_attn):
    x_ref = refs[0]
    attn_refs = refs[1:1 + n_attn]
    wout_refs = refs[1 + n_attn:1 + 2 * n_attn]
    g_ref, wq_ref, k_ref, v_ref, wo_ref, o_ref, h_ref, qn_ref, p_ref, ot_ref = refs[1 + 2 * n_attn:]
    subs = [slice(s * TB, (s + 1) * TB) for s in range(x_ref.shape[0] // TB)]
    hd = CROSS_HEAD_DIM
    for s, rows in enumerate(subs):
        x = x_ref[rows, :]
        for a_ref, w_ref in zip(attn_refs, wout_refs):
            if len(a_ref.shape) == 2:
                a = a_ref[:, rows]
            else:
                a = jnp.concatenate([a_ref[hh, s] for hh in range(a_ref.shape[0])], axis=0)
            a = a.astype(F32).T.astype(BF16)
            x = x + jnp.dot(a, w_ref[...], preferred_element_type=F32)
        o_ref[rows, :] = x
        h_ref[rows, :] = _rms_rows(x, g_ref[...]).astype(BF16)
    scale = hd ** -0.5 * LOG2E
    for cols in subs:
        qt = lax.dot_general(wq_ref[...], h_ref[cols, :], _NT, preferred_element_type=F32)
        for hh in range(CROSS_HEADS):
            blk = qt[hh * hd:(hh + 1) * hd, :]
            qn_ref[hh, :, cols] = (blk * (_rms_scale_cols(blk) * scale)).astype(BF16)
    denoms = {}
    for s, cols in enumerate(subs):
        for hh in range(CROSS_HEADS):
            sc = jnp.dot(k_ref[0, 0, hh], qn_ref[hh, :, cols], preferred_element_type=F32)
            p = jnp.exp2(sc - jnp.max(sc, axis=0, keepdims=True))
            denoms[s, hh] = jnp.sum(p, axis=0, keepdims=True)
            p_ref[hh, :, cols] = p.astype(BF16)
    for s, cols in enumerate(subs):
        for hh in range(CROSS_HEADS):
            o = jnp.dot(v_ref[0, 0, hh], p_ref[hh, :, cols], preferred_element_type=F32)
            ot_ref[hh * hd:(hh + 1) * hd, cols] = o / denoms[s, hh]
    for rows in subs:
        o_ref[rows, :] += jnp.dot(ot_ref[:, rows].T.astype(BF16), wo_ref[...],
                                  preferred_element_type=F32)


def _post_mixer(x, attn_list, wout_list, gain, wq_t, mem_k, mem_v, wo, layer, seq):
    T, D = x.shape
    tile = POST_TILE
    per_seq = seq // tile
    n_attn = len(attn_list)
    const = lambda i: (0, 0)
    in_specs = [pl.BlockSpec((tile, D), lambda i: (i, 0))]
    for a in attn_list:
        if a.ndim == 2:
            in_specs.append(pl.BlockSpec((a.shape[0], tile), lambda i: (0, i)))
        else:
            in_specs.append(pl.BlockSpec((a.shape[0], tile // TB) + a.shape[2:],
                                         lambda i: (0, i, 0, 0)))
    in_specs += [pl.BlockSpec(w.shape, const) for w in wout_list]
    in_specs += [pl.BlockSpec((1, D), const), pl.BlockSpec(wq_t.shape, const),
                 pl.BlockSpec((1, 1) + mem_k.shape[2:], lambda i: (layer, i // per_seq, 0, 0, 0)),
                 pl.BlockSpec((1, 1) + mem_v.shape[2:], lambda i: (layer, i // per_seq, 0, 0, 0)),
                 pl.BlockSpec(wo.shape, const)]
    return pl.pallas_call(
        functools.partial(_post_kernel, n_attn=n_attn),
        grid=(T // tile,),
        in_specs=in_specs,
        out_specs=pl.BlockSpec((tile, D), lambda i: (i, 0)),
        out_shape=jax.ShapeDtypeStruct((T, D), F32),
        scratch_shapes=[pltpu.VMEM((tile, D), BF16),
                        pltpu.VMEM((CROSS_HEADS, CROSS_HEAD_DIM, tile), BF16),
                        pltpu.VMEM((CROSS_HEADS, mem_k.shape[3], tile), BF16),
                        pltpu.VMEM((CROSS_HEADS * CROSS_HEAD_DIM, tile), F32)],
        compiler_params=_params(1),
        name="outproj_cross",
    )(x, *attn_list, *wout_list, gain, wq_t, mem_k, mem_v, wo)


def _ffn_kernel(x_ref, g_ref, w1_ref, cw_ref, w2_ref, o_ref, tail_ref, ext_ref, h_ref, act_ref, *,
                per_seq):
    i = pl.program_id(0)
    tile = x_ref.shape[0]
    n_chunks = w2_ref.shape[0] // FFN_CHUNK
    lead = 8
    x = x_ref[...]
    h_ref[...] = _rms_rows(x, g_ref[...]).astype(BF16)

    @pl.when(i % per_seq == 0)
    def _():
        tail_ref[...] = jnp.zeros(tail_ref.shape, F32)

    o_ref[...] = x

    def project(c, slot):
        for part in range(2):
            cc = c + part * n_chunks
            u = jnp.dot(h_ref[...], w1_ref[:, cc * FFN_CHUNK:(cc + 1) * FFN_CHUNK],
                        preferred_element_type=F32)
            ext_ref[slot, part, 0:lead, :] = tail_ref[cc]
            ext_ref[slot, part, lead:, :] = u
            tail_ref[cc] = u[tile - lead:, :]

    def gate_up(c, slot):
        halves = []
        for part in range(2):
            cc = c + part * n_chunks
            cw = cw_ref[:, cc * FFN_CHUNK:(cc + 1) * FFN_CHUNK]
            ext = ext_ref.at[slot, part]
            halves.append(cw[3:4, :] + cw[0:1, :] * ext[lead - 2:lead - 2 + tile, :]
                          + cw[1:2, :] * ext[lead - 1:lead - 1 + tile, :]
                          + cw[2:3, :] * ext[lead:, :])
        gate, up = halves
        act_ref[c] = (gate * (1.0 / (1.0 + jnp.exp(-gate))) * up).astype(BF16)

    def down(group):
        a = jnp.concatenate([act_ref[c] for c in group], axis=1)
        w = w2_ref[group[0] * FFN_CHUNK:(group[-1] + 1) * FFN_CHUNK, :]
        o_ref[...] += jnp.dot(a, w, preferred_element_type=F32)

    groups = [list(range(s, min(s + FFN_GROUP, n_chunks))) for s in range(0, n_chunks, FFN_GROUP)]

    def slot(gi, k):
        return (gi % 2) * FFN_GROUP + k

    for k, c in enumerate(groups[0]):
        project(c, slot(0, k))
    for gi, group in enumerate(groups):
        if gi + 1 < len(groups):
            for k, c in enumerate(groups[gi + 1]):
                project(c, slot(gi + 1, k))
        for k, c in enumerate(group):
            gate_up(c, slot(gi, k))
        down(group)


def _conv_ffn(x, gain, w1, cw, w2, seq):
    T, D = x.shape
    tile = FFN_TILE
    n_chunks = w2.shape[0] // FFN_CHUNK
    return pl.pallas_call(
        functools.partial(_ffn_kernel, per_seq=seq // tile),
        grid=(T // tile,),
        in_specs=[pl.BlockSpec((tile, D), lambda i: (i, 0)),
                  pl.BlockSpec((1, D), lambda i: (0, 0)),
                  pl.BlockSpec(w1.shape, lambda i: (0, 0), pipeline_mode=pl.Buffered(1)),
                  pl.BlockSpec(cw.shape, lambda i: (0, 0)),
                  pl.BlockSpec(w2.shape, lambda i: (0, 0), pipeline_mode=pl.Buffered(1))],
        out_specs=pl.BlockSpec((tile, D), lambda i: (i, 0)),
        out_shape=jax.ShapeDtypeStruct((T, D), F32),
        scratch_shapes=[pltpu.VMEM((2 * n_chunks, 8, FFN_CHUNK), F32),
                        pltpu.VMEM((min(2 * FFN_GROUP, -(-n_chunks // FFN_GROUP) * FFN_GROUP),
                                    2, tile + 8, FFN_CHUNK), F32),
                        pltpu.VMEM((tile, D), BF16),
                        pltpu.VMEM((n_chunks, tile, FFN_CHUNK), BF16)],
        compiler_params=_params(1),
        name="conv_ffn",
    )(x, gain, w1, cw, w2)


def _t5_bucket(dist):
    n = jnp.maximum(dist, 0)
    max_exact = NUM_BUCKETS // 2
    nf = jnp.maximum(n, 1).astype(F32)
    coef = (NUM_BUCKETS - max_exact) / math.log(MAX_DISTANCE / max_exact)
    large = max_exact + (jnp.log(nf / max_exact) * coef).astype(jnp.int32)
    large = jnp.minimum(large, NUM_BUCKETS - 1)
    return jnp.where(n < max_exact, n, large)


def _toeplitz_kernel(w_ref, o_ref, *, window):
    n_tiles, rows, cols = o_ref.shape[1:]
    period = w_ref.shape[-1]
    key = lax.broadcasted_iota(jnp.int32, (rows, cols), 0)
    qry = lax.broadcasted_iota(jnp.int32, (rows, cols), 1)
    for d in range(n_tiles):
        spread = jnp.broadcast_to(w_ref[0, d], (rows, period))
        skew = pltpu.roll(spread, 0, 1, stride=1, stride_axis=0)
        tile = skew[:, rows:rows + cols]
        if window is not None:
            tile = jnp.where(jnp.logical_and(qry - key + window >= 0, qry - key < 0), tile, NEG_INF)
        elif d == 0:
            tile = jnp.where(qry >= key, tile, NEG_INF)
        o_ref[0, d] = tile


def _toeplitz_tiles(windows, rows, cols, window):
    n_heads, n_tiles, _, period = windows.shape
    assert period >= rows + cols and rows % 128 == 0 and cols % 128 == 0 and period % 128 == 0
    return pl.pallas_call(
        functools.partial(_toeplitz_kernel, window=window),
        grid=(n_heads,),
        in_specs=[pl.BlockSpec((1, n_tiles, 1, period), lambda h: (h, 0, 0, 0))],
        out_specs=pl.BlockSpec((1, n_tiles, rows, cols), lambda h: (h, 0, 0, 0)),
        out_shape=jax.ShapeDtypeStruct((n_heads, n_tiles, rows, cols), F32),
        compiler_params=_params(1),
        name="bias_tiles",
    )(windows)


def _moba_bias_tiles(bias_by_dist):
    n_tiles = min(bias_by_dist.shape[1] // TB, -(-(MAX_DISTANCE + TB) // TB))
    period = 3 * TB
    padded = jnp.pad(bias_by_dist, ((0, 0), (TB, period)))
    windows = jnp.stack([padded[:, d * TB:d * TB + period] for d in range(n_tiles)], axis=1)
    return _toeplitz_tiles(windows[:, :, None, :], TB, TB, None)


def _swa_bias_tiles(bias_by_dist):
    w = SWA_WINDOW
    windows = jnp.pad(bias_by_dist[:, :3 * w], ((0, 0), (w, 0)))
    return _toeplitz_tiles(windows[:, None, None, :], 2 * w, w, w)[:, 0]


def kernel(x, mem, rel_bias, mix_norm, ev_w_in, ev_w_out, ev_q_gain, ev_k_gain, od_w_in, od_w_out, od_q_gain, od_k_gain, od_sinks, cx_norm, cx_mem_norm, cx_w_q, cx_w_kv, cx_w_o, cx_q_gain, cx_k_gain, ff_norm, ff_w_in, ff_conv_w, ff_conv_b, ff_w_out):
    B, S, D = x.shape
    T = B * S
    depth = mix_norm.shape[0]
    n_heads = rel_bias.shape[1]
    sb_heads = n_heads // 2
    d_ff = ff_w_out.shape[1]
    n_chunks = d_ff // FFN_CHUNK
    n_kv = (od_w_in.shape[2] // HEAD_DIM - n_heads) // 2
    assert S % FFN_TILE == 0 and S % POST_TILE == 0 and S % TB == 0 and d_ff % FFN_CHUNK == 0

    def lane_bcast(col, width):
        return jnp.broadcast_to(col[:, None], (col.shape[0], width)).astype(F32)

    bias_by_dist = rel_bias.astype(F32)[_t5_bucket(jnp.arange(S + TB, dtype=jnp.int32))].T
    moba_tiles = _moba_bias_tiles(bias_by_dist[sb_heads:, :S] * LOG2E)
    swa_tiles = _swa_bias_tiles(bias_by_dist * LOG2E)

    cx_kgain = jnp.stack([lane_bcast(cx_k_gain[l] * cx_q_gain[l], mem.shape[1]) for l in range(depth)])
    mem_k, mem_v = _mem_kv(mem, cx_mem_norm[:, None, :], jnp.swapaxes(cx_w_kv, 1, 2).astype(BF16),
                           cx_kgain)

    xf = x.reshape(T, D)
    for layer in range(depth):
        idx = layer // 2
        gain = mix_norm[layer][None, :]
        if layer % 2 == 0:
            kgain = lane_bcast(jnp.tile(ev_k_gain[idx] * ev_q_gain[idx], sb_heads), TB)
            qa, ka, va, qb, kb, vb = _even_inproj(xf, gain, ev_w_in[idx].T.astype(BF16), kgain)
            attn = [_sb_attention(qa, ka, va, B), _moba_attention(qb, kb, vb, moba_tiles, B)]
            half = sb_heads * HEAD_DIM
            wout = [ev_w_out[idx, :half].astype(BF16), ev_w_out[idx, half:].astype(BF16)]
        else:
            kgain = lane_bcast(jnp.tile(od_k_gain[idx] * od_q_gain[idx], n_kv), TB)
            q, k, v = _odd_inproj(xf, gain, od_w_in[idx].T.astype(BF16), kgain, n_heads, n_kv)
            attn = [_swa_attention(q, k, v, swa_tiles, od_sinks[idx].astype(F32), B)]
            wout = [od_w_out[idx].astype(BF16)]
        xf = _post_mixer(xf, attn, wout, cx_norm[layer][None, :], cx_w_q[layer].T.astype(BF16),
                         mem_k, mem_v, cx_w_o[layer].astype(BF16), layer, S)
        cw = jnp.concatenate([ff_conv_w[layer], ff_conv_b[layer][None, :]], axis=0)
        xf = _conv_ffn(xf, ff_norm[layer][None, :], ff_w_in[layer].astype(BF16), cw,
                       ff_w_out[layer].astype(BF16), S)
    return xf.reshape(B, S, D)
```
